```python
import math
import jax, jax.numpy as jnp
from jax import lax
import numpy as np

D_MODEL = 2048
BATCH = 4
SEQ = 4096
DEPTH = 2

EPS = 1e-6
F32 = jnp.float32

SSM_HEAD_DIM = 64
SSM_INNER = D_MODEL
SSM_HEADS = SSM_INNER // SSM_HEAD_DIM
SSM_GROUPS = 4
SSM_STATE = 128
CONV_K = 4
CONV_DIM = SSM_INNER + 2 * SSM_GROUPS * SSM_STATE
SSD_CHUNK = 128

HGRN_EXPAND = 128
HGRN_WIDTH = D_MODEL
HGRN_HEADS = HGRN_WIDTH // HGRN_EXPAND
HGRN_HEAD_K = HGRN_EXPAND
HGRN_HEAD_V = HGRN_WIDTH // HGRN_HEADS
HGRN_QK = HGRN_HEADS * HGRN_HEAD_K
HGRN_CHUNK = 64

N_EXPERTS = 32
N_EXPERT_GROUPS = 4
EXPERTS_PER_GROUP = N_EXPERTS // N_EXPERT_GROUPS
TOP_K = 2
D_FF_EXPERT = D_MODEL // 2
MOE_BLOCK = 128

PROJ_SIZES = (SSM_INNER, CONV_DIM, SSM_HEADS, HGRN_QK, HGRN_QK,
              HGRN_WIDTH, HGRN_WIDTH, D_MODEL, D_MODEL)
N_IN = sum(PROJ_SIZES)

kernel_name = "hybrid_ssd_hgrn2_moe_adaln"


def rms_norm(x, w):
    xf = x.astype(F32)
    y = xf * lax.rsqrt(jnp.mean(xf * xf, axis=-1, keepdims=True) + EPS)
    return (y * w.astype(F32)).astype(x.dtype)


def causal_depthwise_conv(u, w, b):
    ch = u.shape[-1]
    out = lax.conv_general_dilated(
        u, w[:, None, :].astype(u.dtype), window_strides=(1,),
        padding=[(w.shape[0] - 1, 0)],
        dimension_numbers=("NWC", "WIO", "NWC"),
        feature_group_count=ch)
    return out + b.astype(u.dtype)


def ssd_chunked(xh, dt, a, bm, cm):
    b, s, h, p = xh.shape
    g, n = bm.shape[2], bm.shape[3]
    r = h // g
    nc = s // SSD_CHUNK
    q = SSD_CHUNK
    x = xh.reshape(b, nc, q, g, r, p)
    dtc = dt.reshape(b, nc, q, g, r)
    bc = bm.reshape(b, nc, q, g, n)
    cc = cm.reshape(b, nc, q, g, n)
    a_cs = jnp.cumsum(dtc * a.reshape(g, r), axis=2)
    xdt = x * dtc[..., None]

    causal = jnp.tril(jnp.ones((q, q), bool))
    rel = a_cs[:, :, :, None] - a_cs[:, :, None, :]
    lmat = jnp.exp(jnp.where(causal[:, :, None, None], rel, -jnp.inf))
    cb = jnp.einsum("bclgn,bcsgn->bclsg", cc, bc)
    mmat = cb[..., None] * lmat
    y_diag = jnp.einsum("bclsgr,bcsgrp->bclgrp", mmat, xdt)

    decay_states = jnp.exp(a_cs[:, :, -1:] - a_cs)
    states = jnp.einsum("bclgn,bclgr,bclgrp->bcgrpn", bc, decay_states, xdt)
    chunk_decay = jnp.exp(a_cs[:, :, -1])

    def step(carry, inp):
        st, dec = inp
        return carry * dec[..., None, None] + st, carry

    init = jnp.zeros((b, g, r, p, n), F32)
    _, prev = lax.scan(step, init, (jnp.moveaxis(states, 1, 0), jnp.moveaxis(chunk_decay, 1, 0)))
    prev = jnp.moveaxis(prev, 0, 1)

    y_off = jnp.einsum("bclgn,bcgrpn,bclgr->bclgrp", cc, prev, jnp.exp(a_cs))
    return (y_diag + y_off).reshape(b, s, h, p)


def ssd_branch(z, xbc, dt_raw, conv_w, conv_b, dt_bias, a_log, d_skip, norm_w):
    bsz, s, _ = z.shape
    xbc = jax.nn.silu(causal_depthwise_conv(xbc, conv_w, conv_b))
    xs = xbc[..., :SSM_INNER]
    bm = xbc[..., SSM_INNER:SSM_INNER + SSM_GROUPS * SSM_STATE]
    cm = xbc[..., SSM_INNER + SSM_GROUPS * SSM_STATE:]
    dt = jax.nn.softplus(dt_raw.astype(F32) + dt_bias.astype(F32))
    a = -jnp.exp(a_log.astype(F32))
    xh = xs.astype(F32).reshape(bsz, s, SSM_HEADS, SSM_HEAD_DIM)
    y = ssd_chunked(xh, dt, a,
                    bm.astype(F32).reshape(bsz, s, SSM_GROUPS, SSM_STATE),
                    cm.astype(F32).reshape(bsz, s, SSM_GROUPS, SSM_STATE))
    y = y + d_skip.astype(F32)[:, None] * xh
    y = y.reshape(bsz, s, SSM_INNER) * jax.nn.silu(z.astype(F32))
    yg = y.reshape(bsz, s, SSM_GROUPS, SSM_INNER // SSM_GROUPS)
    yg = yg * lax.rsqrt(jnp.mean(yg * yg, axis=-1, keepdims=True) + EPS)
    y = yg.reshape(bsz, s, SSM_INNER) * norm_w.astype(F32)
    return y.astype(z.dtype)


def hgrn2_chunk_scan(q, k, v, logf):
    b, s, h, dk = q.shape
    dv = v.shape[-1]
    nc = s // HGRN_CHUNK

    def to_chunks(t):
        return t.reshape(b, nc, HGRN_CHUNK, h, t.shape[-1]).transpose(1, 0, 3, 2, 4)

    causal = jnp.tril(jnp.ones((HGRN_CHUNK, HGRN_CHUNK), bool))

    def step(state, inp):
        qc, kc, vc, gc = inp
        bcum = jnp.cumsum(gc, axis=2)
        rel = bcum[:, :, :, None, :] - bcum[:, :, None, :, :]
        decay = jnp.exp(jnp.where(causal[:, :, None], rel, -jnp.inf))
        scores = jnp.einsum("bhtsk,bhtk,bhsk->bhts", decay, qc, kc)
        o = (jnp.einsum("bhts,bhsv->bhtv", scores, vc)
             + jnp.einsum("bhtk,bhkv->bhtv", qc * jnp.exp(bcum), state))
        b_last = bcum[:, :, -1, :]
        new_state = (jnp.exp(b_last)[..., None] * state
                     + jnp.einsum("bhsk,bhsv->bhkv", kc * jnp.exp(b_last[:, :, None, :] - bcum), vc))
        return new_state, o

    s0 = jnp.zeros((b, h, dk, dv), F32)
    _, o = lax.scan(step, s0, (to_chunks(q), to_chunks(k), to_chunks(v), to_chunks(logf)))
    return o.transpose(1, 0, 3, 2, 4).reshape(b, s, h, dv)


def hgrn2_branch(q, f, i, g, lb, norm_w):
    bsz, s, _ = q.shape
    lb = jnp.maximum(lb, 0.0)
    logf = jnp.logaddexp(jnp.log(lb), jnp.log1p(-lb) + jax.nn.log_sigmoid(f.astype(F32)))
    k = -jnp.expm1(logf)
    qf = jax.nn.silu(q.astype(F32))
    hd = lambda t, d: t.reshape(bsz, s, HGRN_HEADS, d)
    o = hgrn2_chunk_scan(hd(qf, HGRN_HEAD_K), hd(k, HGRN_HEAD_K),
                         hd(i.astype(F32), HGRN_HEAD_V), hd(logf, HGRN_HEAD_K))
    o = o * lax.rsqrt(jnp.mean(o * o, axis=-1, keepdims=True) + EPS)
    o = o.reshape(bsz, s, HGRN_WIDTH) * norm_w.astype(F32) * jax.nn.silu(g.astype(F32))
    return o.astype(q.dtype)


def hybrid_mixer(h, w_in, conv_w, conv_b, dt_bias, a_log, d_skip, ssm_norm_w,
                 lb, hgrn_norm_w, w_out):
    proj = jnp.matmul(h, w_in)
    splits = [int(v) for v in np.cumsum(PROJ_SIZES)[:-1]]
    z, xbc, dt_raw, q, f, i, g, gate_a, gate_b = jnp.split(proj, splits, axis=-1)
    y_a = ssd_branch(z, xbc, dt_raw, conv_w, conv_b, dt_bias, a_log, d_skip, ssm_norm_w)
    y_b = hgrn2_branch(q, f, i, g, lb, hgrn_norm_w)
    merged = jax.nn.sigmoid(gate_a) * y_a + jax.nn.sigmoid(gate_b) * y_b
    return jnp.matmul(merged, w_out)


def moe_ffn(h, router_w, router_bias, w_gate, w_up, w_down):
    n_tok, d = h.shape
    scores = jax.nn.sigmoid(jnp.matmul(h.astype(F32), router_w.astype(F32)))
    sel = scores + router_bias.astype(F32)
    grouped = sel.reshape(n_tok, N_EXPERT_GROUPS, EXPERTS_PER_GROUP)
    group_score = lax.top_k(grouped, TOP_K)[0].sum(-1)
    g_idx = jnp.argmax(group_score, axis=-1)
    in_group = jnp.take_along_axis(grouped, g_idx[:, None, None], axis=1)[:, 0]
    _, local = lax.top_k(in_group, TOP_K)
    expert_idx = g_idx[:, None] * EXPERTS_PER_GROUP + local
    gate_w = jnp.take_along_axis(scores, expert_idx, axis=1)
    gate_w = gate_w / jnp.sum(gate_w, axis=-1, keepdims=True)

    n_assign = n_tok * TOP_K
    flat_e = expert_idx.reshape(-1)
    flat_tok = jnp.repeat(jnp.arange(n_tok, dtype=jnp.int32), TOP_K)
    flat_w = gate_w.reshape(-1)
    order = jnp.argsort(flat_e)
    sorted_e = flat_e[order]
    counts = jnp.bincount(flat_e, length=N_EXPERTS)
    starts = jnp.cumsum(counts) - counts
    padded = (counts + MOE_BLOCK - 1) // MOE_BLOCK * MOE_BLOCK
    padded_end = jnp.cumsum(padded)
    padded_start = padded_end - padded
    dest = padded_start[sorted_e] + jnp.arange(n_assign) - starts[sorted_e]
    n_blocks = -(-n_assign // MOE_BLOCK) + N_EXPERTS
    n_rows = n_blocks * MOE_BLOCK
    row_tok = jnp.zeros((n_rows,), jnp.int32).at[dest].set(flat_tok[order])
    row_w = jnp.zeros((n_rows,), F32).at[dest].set(flat_w[order])
    block_e = jnp.minimum(
        jnp.searchsorted(padded_end, jnp.arange(n_blocks) * MOE_BLOCK, side="right"),
        N_EXPERTS - 1)
    xb = h[row_tok].reshape(n_blocks, MOE_BLOCK, d)

    def expert_block(args):
        xblk, e = args
        hid = jax.nn.silu(jnp.matmul(xblk, w_gate[e])) * jnp.matmul(xblk, w_up[e])
        return jnp.matmul(hid, w_down[e])

    yb = lax.map(expert_block, (xb, block_e)).reshape(n_rows, d)
    out = jax.ops.segment_sum(yb.astype(F32) * row_w[:, None], row_tok, num_segments=n_tok)
    return out.astype(h.dtype)


def setup_inputs(seed: int = 0) -> dict:
    key = jax.random.key(seed)
    ks = jax.random.split(key, 24)
    nrm = jax.random.normal
    L, D, E, F = DEPTH, D_MODEL, N_EXPERTS, D_FF_EXPERT
    dt0 = jnp.exp(jax.random.uniform(ks[9], (L, SSM_HEADS)) * (math.log(0.1) - math.log(0.001))
                  + math.log(0.001))
    return {
        "x": nrm(ks[0], (BATCH, SEQ, D), F32),
        "c": nrm(ks[1], (BATCH, D), F32),
        "ada_w": nrm(ks[2], (L, D, 6 * D), F32) * (0.5 * D ** -0.5),
        "ada_b": nrm(ks[3], (L, 6 * D), F32) * 0.02,
        "norm1_w": 1.0 + 0.05 * nrm(ks[4], (L, D), F32),
        "norm2_w": 1.0 + 0.05 * nrm(ks[5], (L, D), F32),
        "w_in": nrm(ks[6], (L, D, N_IN), F32) * D ** -0.5,
        "conv_w": nrm(ks[7], (L, CONV_K, CONV_DIM), F32) * CONV_K ** -0.5,
        "conv_b": nrm(ks[8], (L, CONV_DIM), F32) * 0.02,
        "dt_bias": dt0 + jnp.log(-jnp.expm1(-dt0)),
        "a_log": jnp.log(jax.random.uniform(ks[10], (L, SSM_HEADS), F32, 1.0, 16.0)),
        "d_skip": 1.0 + 0.1 * nrm(ks[11], (L, SSM_HEADS), F32),
        "ssm_norm_w": 1.0 + 0.05 * nrm(ks[12], (L, SSM_INNER), F32),
        "hgrn_lower_bounds": 0.5 * nrm(ks[13], (L, HGRN_QK), F32),
        "hgrn_norm_w": 1.0 + 0.05 * nrm(ks[14], (L, HGRN_WIDTH), F32),
        "w_out": nrm(ks[15], (L, D, D), F32) * D ** -0.5,
        "router_w": nrm(ks[16], (D, E), F32) * D ** -0.5,
        "router_bias": 0.01 * nrm(ks[17], (E,), F32),
        "w_gate": nrm(ks[18], (L, E, D, F), F32) * D ** -0.5,
        "w_up": nrm(ks[19], (L, E, D, F), F32) * D ** -0.5,
        "w_down": nrm(ks[20], (L, E, F, D), F32) * F ** -0.5,
        "final_norm_w": 1.0 + 0.05 * nrm(ks[21], (D,), F32),
    }


def reference(x, c, ada_w, ada_b, norm1_w, norm2_w, w_in, conv_w, conv_b, dt_bias,
              a_log, d_skip, ssm_norm_w, hgrn_lower_bounds, hgrn_norm_w, w_out,
              router_w, router_bias, w_gate, w_up, w_down, final_norm_w):
    bsz, s, d = x.shape
    lb_all = jnp.cumsum(jax.nn.softmax(hgrn_lower_bounds.astype(F32), axis=0), axis=0)
    lb_all = lb_all - lb_all[0:1]
    c_act = jax.nn.silu(c)
    for l in range(DEPTH):
        mod = jnp.matmul(c_act, ada_w[l]) + ada_b[l]
        sh1, sc1, g1, sh2, sc2, g2 = jnp.split(mod, 6, axis=-1)
        h = rms_norm(x, norm1_w[l]) * (1.0 + sc1[:, None, :]) + sh1[:, None, :]
        mix = hybrid_mixer(h, w_in[l], conv_w[l], conv_b[l], dt_bias[l], a_log[l],
                           d_skip[l], ssm_norm_w[l], lb_all[l], hgrn_norm_w[l], w_out[l])
        x = x + g1[:, None, :] * mix
        h = rms_norm(x, norm2_w[l]) * (1.0 + sc2[:, None, :]) + sh2[:, None, :]
        ffn = moe_ffn(h.reshape(bsz * s, d), router_w, router_bias,
                      w_gate[l], w_up[l], w_down[l]).reshape(bsz, s, d)
        x = x + g2[:, None, :] * ffn
    return rms_norm(x, final_norm_w)
```

```python
import functools

import jax
import jax.numpy as jnp
from jax import lax
from jax.experimental import pallas as pl
from jax.experimental.pallas import tpu as pltpu

F32 = jnp.float32
BF16 = jnp.bfloat16
EPS = 1e-6

SSM_HEAD_DIM = 64
SSM_STATE = 128
SSM_GROUPS = 4
CONV_K = 4
HGRN_HEAD = 128
N_EXPERT_GROUPS = 4
EXPERTS_PER_GROUP = 8
LANES = 128
SUBLANES = 8

SSD_CHUNK = 128
HGRN_CHUNK = 128
HGRN_HEADS_PER_STEP = 4
HGRN_CHUNKS_PER_STEP = 2
MOE_ROWS = 256
INPROJ_TM = 1024
INPROJ_TN = 1024
OUT_TM = 256
TOKEN_TILE = 256
VMEM_LIMIT = 56 * 1024 * 1024


def _cparams(sem):
    return pltpu.CompilerParams(dimension_semantics=sem, vmem_limit_bytes=VMEM_LIMIT)


def _sigmoid(x):
    return 1.0 / (1.0 + jnp.exp(-x))


def _silu(x):
    return x * _sigmoid(x)


def _split3(x):
    hi = x.astype(BF16)
    r1 = x - hi.astype(F32)
    mid = r1.astype(BF16)
    lo = (r1 - mid.astype(F32)).astype(BF16)
    return hi, mid, lo


def _dot(a, b):
    return jnp.dot(a, b, preferred_element_type=F32)


def _dot_nt(a, b):
    return lax.dot_general(a, b, (((1,), (1,)), ((), ())), preferred_element_type=F32)


def _sel_dot(sel_bf16, x):
    hi, mid, lo = _split3(x)
    return _dot(sel_bf16, hi) + (_dot(sel_bf16, mid) + _dot(sel_bf16, lo))


def _dot_sel(x, sel_bf16):
    hi, mid, lo = _split3(x)
    return _dot(hi, sel_bf16) + (_dot(mid, sel_bf16) + _dot(lo, sel_bf16))


def _tril_ones(n):
    r = lax.broadcasted_iota(jnp.int32, (n, n), 0)
    c = lax.broadcasted_iota(jnp.int32, (n, n), 1)
    return jnp.where(r >= c, 1.0, 0.0).astype(BF16)


def _mod_kernel(c_ref, w_ref, b_ref, o_ref):
    ca = _silu(c_ref[...]).astype(BF16)
    o_ref[0] = _dot(ca, w_ref[0].astype(BF16)) + b_ref[0]


def _modulation(c, ada_w, ada_b):
    n_layers, d, n6 = ada_w.shape
    bsz = c.shape[0]
    tn = 1024
    return pl.pallas_call(
        _mod_kernel,
        grid=(n_layers, n6 // tn),
        in_specs=[
            pl.BlockSpec((bsz, d), lambda l, j: (0, 0)),
            pl.BlockSpec((1, d, tn), lambda l, j: (l, 0, j)),
            pl.BlockSpec((1, 1, tn), lambda l, j: (l, 0, j)),
        ],
        out_specs=pl.BlockSpec((1, bsz, tn), lambda l, j: (l, 0, j)),
        out_shape=jax.ShapeDtypeStruct((n_layers, bsz, n6), F32),
        compiler_params=_cparams(("arbitrary", "arbitrary")),
        name="adaln_mod",
    )(c, ada_w, ada_b.reshape(n_layers, 1, n6))


def _rms_mod(x, nw, sc, sh):
    ms = jnp.mean(x * x, axis=-1, keepdims=True)
    return (x * lax.rsqrt(ms + EPS) * nw) * (1.0 + sc) + sh


def _inproj_kernel(x_ref, nw_ref, sc_ref, sh_ref, w_ref, wdt_ref, o_ref, dt_ref, h_scr):
    @pl.when(pl.program_id(1) == 0)
    def _():
        h = _rms_mod(x_ref[...], nw_ref[...], sc_ref[0], sh_ref[0]).astype(BF16)
        h_scr[...] = h
        dt_ref[...] = _dot(h, wdt_ref[...])

    o_ref[...] = _dot(h_scr[...], w_ref[...])


def _in_projection(x2, nw, sc, sh, w_r, w_dt, seq):
    n, d = x2.shape
    np_ = w_r.shape[1]
    tm = min(INPROJ_TM, seq)
    tn = INPROJ_TN
    tiles_per_batch = seq // tm
    return pl.pallas_call(
        _inproj_kernel,
        grid=(n // tm, np_ // tn),
        in_specs=[
            pl.BlockSpec((tm, d), lambda i, j: (i, 0)),
            pl.BlockSpec((1, d), lambda i, j: (0, 0)),
            pl.BlockSpec((1, 1, d), lambda i, j: (i // tiles_per_batch, 0, 0)),
            pl.BlockSpec((1, 1, d), lambda i, j: (i // tiles_per_batch, 0, 0)),
            pl.BlockSpec((d, tn), lambda i, j: (0, j)),
            pl.BlockSpec((d, LANES), lambda i, j: (0, 0)),
        ],
        out_specs=[
            pl.BlockSpec((tm, tn), lambda i, j: (i, j)),
            pl.BlockSpec((tm, LANES), lambda i, j: (i, 0)),
        ],
        out_shape=[
            jax.ShapeDtypeStruct((n, np_), F32),
            jax.ShapeDtypeStruct((n, LANES), F32),
        ],
        scratch_shapes=[pltpu.VMEM((tm, d), BF16)],
        compiler_params=_cparams(("arbitrary", "arbitrary")),
        name="norm_inproj",
    )(x2, nw, sc, sh, w_r, w_dt)


def _causal_conv(ext_ref, u, w_ref, b_ref, first):
    q = u.shape[0]

    @pl.when(first)
    def _():
        ext_ref[0:SUBLANES, :] = jnp.zeros((SUBLANES, u.shape[1]), F32)

    ext_ref[SUBLANES:SUBLANES + q, :] = u
    acc = b_ref[...] + w_ref[CONV_K - 1:CONV_K, :] * u
    for k in range(CONV_K - 1):
        off = SUBLANES - (CONV_K - 1) + k
        acc = acc + w_ref[k:k + 1, :] * ext_ref[off:off + q, :]
    ext_ref[0:SUBLANES, :] = ext_ref[q:q + SUBLANES, :]
    return _silu(acc)


def _ssd_kernel(z_ref, xs_ref, bm_ref, cm_ref, dt_ref,
                cwx_ref, cwb_ref, cwc_ref, cbx_ref, cbb_ref, cbc_ref,
                dtb_ref, alog_ref, dskip_ref, nw_ref, exp_ref,
                o_ref,
                state_scr, extx_scr, extb_scr, extc_scr):
    q = z_ref.shape[0]
    d_inner = z_ref.shape[1]
    gw = d_inner // SSM_GROUPS
    heads_per_group = gw // SSM_HEAD_DIM
    first = pl.program_id(1) == 0

    @pl.when(first)
    def _():
        state_scr[...] = jnp.zeros(state_scr.shape, F32)

    xs = _causal_conv(extx_scr, xs_ref[...], cwx_ref, cbx_ref, first)
    bmat = _causal_conv(extb_scr, bm_ref[...], cwb_ref, cbb_ref, first)
    cmat = _causal_conv(extc_scr, cm_ref[...], cwc_ref, cbc_ref, first)

    x_dt = dt_ref[...] + dtb_ref[...]
    dt = jnp.maximum(x_dt, 0.0) + jnp.log1p(jnp.exp(-jnp.abs(x_dt)))
    a = -jnp.exp(alog_ref[...])
    acs = _sel_dot(_tril_ones(q), dt * a)
    acs_last = acs[q - 1:q, :]
    stack = jnp.concatenate(
        [dt, jnp.exp(acs), jnp.exp(acs_last - acs),
         jnp.broadcast_to(jnp.exp(acs_last), (SUBLANES, LANES))], axis=0)
    wide = _dot_sel(stack, exp_ref[...])
    dt_e = wide[0:q]
    eacs_e = wide[q:2 * q]
    dec_e = wide[2 * q:3 * q]
    cd_e = wide[3 * q:3 * q + 1]

    xdt = xs * dt_e
    xdt_b = xdt.astype(BF16)
    xdec_b = (xdt * dec_e).astype(BF16)

    acs_t = acs.T
    ri = lax.broadcasted_iota(jnp.int32, (q, q), 0)
    ci = lax.broadcasted_iota(jnp.int32, (q, q), 1)
    causal = ri >= ci
    lane = lax.broadcasted_iota(jnp.int32, (q, LANES), 1)
    lo_half = lane < SSM_HEAD_DIM

    y_parts = []
    for g in range(SSM_GROUPS):
        bg = bmat[:, g * SSM_STATE:(g + 1) * SSM_STATE]
        cg = cmat[:, g * SSM_STATE:(g + 1) * SSM_STATE]
        bg_b = bg.astype(BF16)
        cg_b = cg.astype(BF16)
        cb = _dot_nt(cg_b, bg_b)
        st = state_scr[g]
        y_off = _dot(cg_b, st.astype(BF16)) * eacs_e[:, g * gw:(g + 1) * gw]
        diag = []
        for pair in range(heads_per_group // 2):
            ms = []
            for sub in range(2):
                h = g * heads_per_group + pair * 2 + sub
                rel = acs[:, h:h + 1] - acs_t[h:h + 1, :]
                lmat = jnp.exp(jnp.where(causal, rel, -jnp.inf))
                ms.append((cb * lmat).astype(BF16))
            lhs = jnp.concatenate(ms, axis=1)
            c0 = (g * heads_per_group + pair * 2) * SSM_HEAD_DIM
            xp = xdt_b[:, c0:c0 + LANES]
            zero = jnp.zeros_like(xp)
            rhs = jnp.concatenate(
                [jnp.where(lo_half, xp, zero), jnp.where(lo_half, zero, xp)], axis=0)
            diag.append(_dot(lhs, rhs))
        y_parts.append(jnp.concatenate(diag, axis=1) + y_off)
        new_state = st * cd_e[:, g * gw:(g + 1) * gw] + _dot(
            bg.T.astype(BF16), xdec_b[:, g * gw:(g + 1) * gw])
        state_scr[g] = new_state

    y = jnp.concatenate(y_parts, axis=1) + dskip_ref[...] * xs
    y = y * _silu(z_ref[...])
    outs = []
    for g in range(SSM_GROUPS):
        yg = y[:, g * gw:(g + 1) * gw]
        ms = jnp.mean(yg * yg, axis=-1, keepdims=True)
        outs.append(yg * lax.rsqrt(ms + EPS))
    o_ref[...] = (jnp.concatenate(outs, axis=1) * nw_ref[...]).astype(o_ref.dtype)


def _ssd_branch(proj, dt_raw, p, bsz, seq, d):
    q = min(SSD_CHUNK, seq)
    nc = seq // q
    gn = SSM_GROUPS * SSM_STATE
    b_blk = 8 * d // gn
    row = lambda b, c: b * nc + c
    vec = lambda width: pl.BlockSpec((1, width), lambda b, c: (0, 0))
    return pl.pallas_call(
        _ssd_kernel,
        grid=(bsz, nc),
        in_specs=[
            pl.BlockSpec((q, d), lambda b, c: (row(b, c), 0)),
            pl.BlockSpec((q, d), lambda b, c: (row(b, c), 1)),
            pl.BlockSpec((q, gn), lambda b, c: (row(b, c), b_blk)),
            pl.BlockSpec((q, gn), lambda b, c: (row(b, c), b_blk + 1)),
            pl.BlockSpec((q, LANES), lambda b, c: (row(b, c), 0)),
            pl.BlockSpec((CONV_K, d), lambda b, c: (0, 0)),
            pl.BlockSpec((CONV_K, gn), lambda b, c: (0, 0)),
            pl.BlockSpec((CONV_K, gn), lambda b, c: (0, 0)),
            vec(d), vec(gn), vec(gn),
            vec(LANES), vec(LANES), vec(d), vec(d),
            pl.BlockSpec((LANES, d), lambda b, c: (0, 0)),
        ],
        out_specs=pl.BlockSpec((q, d), lambda b, c: (row(b, c), 0)),
        out_shape=jax.ShapeDtypeStruct((bsz * seq, d), BF16),
        scratch_shapes=[
            pltpu.VMEM((SSM_GROUPS, SSM_STATE, d // SSM_GROUPS), F32),
            pltpu.VMEM((q + SUBLANES, d), F32),
            pltpu.VMEM((q + SUBLANES, gn), F32),
            pltpu.VMEM((q + SUBLANES, gn), F32),
        ],
        compiler_params=_cparams(("arbitrary", "arbitrary")),
        name="ssd_scan",
    )(proj, proj, proj, proj, dt_raw,
      p["cwx"], p["cwb"], p["cwc"], p["cbx"], p["cbb"], p["cbc"],
      p["dt_bias"], p["a_log"], p["d_skip"], p["ssm_norm_w"], p["head_expand"])


def _hgrn_chunk(q_raw, f_raw, v, g_raw, loglb, log1mlb, nw, s_t):
    c = q_raw.shape[0]
    logsig = jnp.minimum(f_raw, 0.0) - jnp.log1p(jnp.exp(-jnp.abs(f_raw)))
    bb = log1mlb + logsig
    logf = jnp.maximum(loglb, bb) + jnp.log1p(jnp.exp(-jnp.abs(loglb - bb)))
    kk = jnp.exp(bb - f_raw)
    qf = _silu(q_raw)
    v_b = v.astype(BF16)

    b = _sel_dot(_tril_ones(c), logf)
    row = lax.broadcasted_iota(jnp.int32, (c, LANES), 0)
    sub = row & (SUBLANES - 1)
    b3 = b.reshape(c // SUBLANES, SUBLANES, LANES)

    def bcast_row(i):
        return jnp.broadcast_to(b3[:, i:i + 1, :], b3.shape).reshape(c, LANES)

    a_levels = []
    w = 1
    while w < c:
        if w == 1:
            bm = jnp.where(sub < 2, bcast_row(0),
                           jnp.where(sub < 4, bcast_row(2),
                                     jnp.where(sub < 6, bcast_row(4), bcast_row(6))))
        elif w == 2:
            bm = jnp.where(sub < 4, bcast_row(1), bcast_row(5))
        else:
            bw = b.reshape(c // (2 * w), 2 * w, LANES)
            bm = jnp.broadcast_to(bw[:, w - 1:w, :], bw.shape).reshape(c, LANES)
        upper = (row & w) != 0
        d = b - bm
        e = jnp.exp(jnp.where(upper, d, -d))
        qt = jnp.where(upper, qf * e, 0.0).astype(BF16)
        kt = jnp.where(upper, 0.0, kk * e).astype(BF16)
        a_levels.append((w, _dot_nt(qt, kt)))
        w *= 2
    ri = lax.broadcasted_iota(jnp.int32, (c, c), 0)
    ci = lax.broadcasted_iota(jnp.int32, (c, c), 1)
    x = ri ^ ci
    scores = a_levels[-1][1]
    for w, a_w in reversed(a_levels[:-1]):
        scores = jnp.where(x < 2 * w, a_w, scores)

    o = _dot(scores.astype(BF16), v_b)
    o = o + jnp.sum(qf * kk, axis=-1, keepdims=True) * v
    o = o + _dot_nt((qf * jnp.exp(b)).astype(BF16), s_t.astype(BF16))
    b_last = b[c - 1:c, :]
    kst = (kk * jnp.exp(b_last - b)).astype(BF16)
    s_new = s_t * jnp.exp(b_last) + _dot(v.T.astype(BF16), kst)

    ms = jnp.mean(o * o, axis=-1, keepdims=True)
    y = o * lax.rsqrt(ms + EPS) * nw * _silu(g_raw)
    return y, s_new


def _hgrn_kernel(q_ref, f_ref, i_ref, g_ref, loglb_ref, log1mlb_ref, nw_ref, o_ref, state_scr):
    rows = q_ref.shape[0]
    heads = q_ref.shape[1] // HGRN_HEAD
    c = min(HGRN_CHUNK, rows)

    @pl.when(pl.program_id(2) == 0)
    def _():
        state_scr[...] = jnp.zeros(state_scr.shape, F32)

    for h in range(heads):
        cols = slice(h * HGRN_HEAD, (h + 1) * HGRN_HEAD)
        s_t = state_scr[h]
        for k in range(rows // c):
            rs = slice(k * c, (k + 1) * c)
            y, s_t = _hgrn_chunk(q_ref[rs, cols], f_ref[rs, cols], i_ref[rs, cols], g_ref[rs, cols],
                                 loglb_ref[:, cols], log1mlb_ref[:, cols], nw_ref[:, cols], s_t)
            o_ref[rs, cols] = y.astype(o_ref.dtype)
        state_scr[h] = s_t


def _hgrn_branch(proj, p, bsz, seq, d):
    n_heads = d // HGRN_HEAD
    hb = min(HGRN_HEADS_PER_STEP, n_heads)
    wblk = hb * HGRN_HEAD
    rows = min(HGRN_CHUNK * HGRN_CHUNKS_PER_STEP, seq)
    nt = seq // rows
    per = d // wblk

    def sect(k):
        return pl.BlockSpec((rows, wblk), lambda b, hg, t: (b * nt + t, k * per + hg))

    vec = pl.BlockSpec((1, wblk), lambda b, hg, t: (0, hg))
    return pl.pallas_call(
        _hgrn_kernel,
        grid=(bsz, n_heads // hb, nt),
        in_specs=[sect(2), sect(3), sect(4), sect(5), vec, vec, vec],
        out_specs=pl.BlockSpec((rows, wblk), lambda b, hg, t: (b * nt + t, hg)),
        out_shape=jax.ShapeDtypeStruct((bsz * seq, d), BF16),
        scratch_shapes=[pltpu.VMEM((hb, HGRN_HEAD, HGRN_HEAD), F32)],
        compiler_params=_cparams(("arbitrary", "arbitrary", "arbitrary")),
        name="hgrn2_scan",
    )(proj, proj, proj, proj, p["log_lb"], p["log_1m_lb"], p["hgrn_norm_w"])


def _first_argmax(vals, idx, n):
    m = jnp.max(vals, axis=0, keepdims=True)
    i = jnp.min(jnp.where(vals == m, idx, float(n)), axis=0, keepdims=True)
    return m, i


def _outproj_kernel(x_ref, ya_ref, yb_ref, ga_ref, gb_ref, wout_ref, g1_ref,
                    nw2_ref, sc2_ref, sh2_ref, rwt_ref, rb_ref,
                    xo_ref, h2_ref, mi_ref, mw_ref, cnt_ref, cnt_scr):
    tm = x_ref.shape[0]
    n_exp = rwt_ref.shape[0]

    @pl.when(pl.program_id(0) == 0)
    def _():
        cnt_scr[...] = jnp.zeros(cnt_scr.shape, F32)

    merged = (_sigmoid(ga_ref[...]) * ya_ref[...].astype(F32)
              + _sigmoid(gb_ref[...]) * yb_ref[...].astype(F32))
    mix = _dot(merged.astype(BF16), wout_ref[...])
    xn = x_ref[...] + g1_ref[0] * mix
    xo_ref[...] = xn
    h2 = _rms_mod(xn, nw2_ref[...], sc2_ref[0], sh2_ref[0])
    h2_ref[...] = h2

    h_hi, h_mid, h_lo = _split3(h2)
    r_hi, r_mid, r_lo = _split3(rwt_ref[...])
    logits = (_dot_nt(r_hi, h_hi) + (_dot_nt(r_hi, h_mid) + _dot_nt(r_mid, h_hi))
              + (_dot_nt(r_hi, h_lo) + _dot_nt(r_lo, h_hi) + _dot_nt(r_mid, h_mid)))
    scores = _sigmoid(logits)
    sel = scores + rb_ref[...]

    sub = lax.broadcasted_iota(jnp.int32, (EXPERTS_PER_GROUP, tm), 0).astype(F32)
    neg = -jnp.inf
    best = None
    for g in range(N_EXPERT_GROUPS):
        blk = sel[g * EXPERTS_PER_GROUP:(g + 1) * EXPERTS_PER_GROUP]
        m1, i1 = _first_argmax(blk, sub, EXPERTS_PER_GROUP)
        m2 = jnp.max(jnp.where(sub == i1, neg, blk), axis=0, keepdims=True)
        gs = m1 + m2
        if best is None:
            best, gi = gs, jnp.zeros_like(gs)
            in_sel, in_sc = blk, scores[0:EXPERTS_PER_GROUP]
        else:
            upd = gs > best
            best = jnp.where(upd, gs, best)
            gi = jnp.where(upd, float(g), gi)
            in_sel = jnp.where(upd, blk, in_sel)
            in_sc = jnp.where(upd, scores[g * EXPERTS_PER_GROUP:(g + 1) * EXPERTS_PER_GROUP], in_sc)
    _, l1 = _first_argmax(in_sel, sub, EXPERTS_PER_GROUP)
    _, l2 = _first_argmax(jnp.where(sub == l1, neg, in_sel), sub, EXPERTS_PER_GROUP)
    s1 = jnp.sum(jnp.where(sub == l1, in_sc, 0.0), axis=0, keepdims=True)
    s2 = jnp.sum(jnp.where(sub == l2, in_sc, 0.0), axis=0, keepdims=True)
    e1 = gi * EXPERTS_PER_GROUP + l1
    e2 = gi * EXPERTS_PER_GROUP + l2
    mw_ref[0:1, :] = s1 / (s1 + s2)
    mw_ref[1:2, :] = s2 / (s1 + s2)

    eio = lax.broadcasted_iota(jnp.int32, (n_exp, tm), 0).astype(F32)
    oh1 = eio == e1
    oh2 = eio == e2
    onehot = jnp.where(oh1 | oh2, 1.0, 0.0)
    tr = lax.broadcasted_iota(jnp.int32, (tm, tm), 0)
    tc = lax.broadcasted_iota(jnp.int32, (tm, tm), 1)
    before = jnp.where(tr < tc, 1.0, 0.0).astype(BF16)
    pref = _dot(onehot.astype(BF16), before) + cnt_scr[...]
    r1 = jnp.sum(jnp.where(oh1, pref, 0.0), axis=0, keepdims=True)
    r2 = jnp.sum(jnp.where(oh2, pref, 0.0), axis=0, keepdims=True)
    mi_ref[0:1, :] = e1.astype(jnp.int32)
    mi_ref[1:2, :] = e2.astype(jnp.int32)
    mi_ref[2:3, :] = r1.astype(jnp.int32)
    mi_ref[3:4, :] = r2.astype(jnp.int32)
    total = cnt_scr[...] + jnp.sum(onehot, axis=1, keepdims=True)
    cnt_scr[...] = total
    cnt_ref[...] = jnp.broadcast_to(total, cnt_ref.shape)


def _out_projection(x2, ya, yb, proj, w_out_b, g1, nw2, sc2, sh2, rwt, rb, seq):
    n, d = x2.shape
    n_exp = rwt.shape[0]
    tm = min(OUT_TM, seq)
    tpb = seq // tm
    bvec = pl.BlockSpec((1, 1, d), lambda i: (i // tpb, 0, 0))
    return pl.pallas_call(
        _outproj_kernel,
        grid=(n // tm,),
        in_specs=[
            pl.BlockSpec((tm, d), lambda i: (i, 0)),
            pl.BlockSpec((tm, d), lambda i: (i, 0)),
            pl.BlockSpec((tm, d), lambda i: (i, 0)),
            pl.BlockSpec((tm, d), lambda i: (i, 6)),
            pl.BlockSpec((tm, d), lambda i: (i, 7)),
            pl.BlockSpec((d, d), lambda i: (0, 0)),
            bvec,
            pl.BlockSpec((1, d), lambda i: (0, 0)),
            bvec, bvec,
            pl.BlockSpec((n_exp, d), lambda i: (0, 0)),
            pl.BlockSpec((n_exp, 1), lambda i: (0, 0)),
        ],
        out_specs=[
            pl.BlockSpec((tm, d), lambda i: (i, 0)),
            pl.BlockSpec((tm, d), lambda i: (i, 0)),
            pl.BlockSpec((4, tm), lambda i: (0, i)),
            pl.BlockSpec((2, tm), lambda i: (0, i)),
            pl.BlockSpec((n_exp, LANES), lambda i: (0, 0)),
        ],
        out_shape=[
            jax.ShapeDtypeStruct((n, d), F32),
            jax.ShapeDtypeStruct((n, d), F32),
            jax.ShapeDtypeStruct((4, n), jnp.int32),
            jax.ShapeDtypeStruct((2, n), F32),
            jax.ShapeDtypeStruct((n_exp, LANES), F32),
        ],
        scratch_shapes=[pltpu.VMEM((n_exp, 1), F32)],
        compiler_params=_cparams(("arbitrary",)),
        name="merge_outproj_router",
    )(x2, ya, yb, proj, proj, w_out_b, g1, nw2, sc2, sh2, rwt, rb)


def _row_copy(src_ref, src_row, dst_ref, dst_row, sem):
    return pltpu.make_async_copy(src_ref.at[pl.ds(src_row, 1)], dst_ref.at[pl.ds(dst_row, 1)], sem)


def _dispatch_kernel(dest_ref, h_ref, xb_in_ref, xb_ref, sem):
    del xb_in_ref
    tm = h_ref.shape[0]

    def start(t, carry):
        _row_copy(h_ref, t, xb_ref, dest_ref[0, t], sem).start()
        _row_copy(h_ref, t, xb_ref, dest_ref[1, t], sem).start()
        return carry

    lax.fori_loop(0, tm, start, 0)

    def wait(t, carry):
        _row_copy(h_ref, 0, xb_ref, 0, sem).wait()
        _row_copy(h_ref, 0, xb_ref, 0, sem).wait()
        return carry

    lax.fori_loop(0, tm, wait, 0)


def _dispatch(dest, h2, n_rows):
    n, d = h2.shape
    tm = min(TOKEN_TILE, n)
    xb0 = jnp.zeros((n_rows, d), F32)
    return pl.pallas_call(
        _dispatch_kernel,
        grid=(n // tm,),
        in_specs=[
            pl.BlockSpec((2, tm), lambda i: (0, i), memory_space=pltpu.SMEM),
            pl.BlockSpec((tm, d), lambda i: (i, 0)),
            pl.BlockSpec(memory_space=pl.ANY),
        ],
        out_specs=pl.BlockSpec(memory_space=pl.ANY),
        out_shape=jax.ShapeDtypeStruct((n_rows, d), F32),
        scratch_shapes=[pltpu.SemaphoreType.DMA(())],
        input_output_aliases={2: 0},
        compiler_params=_cparams(("arbitrary",)),
        name="moe_dispatch",
    )(dest, h2, xb0)


def _expert_kernel(be_ref, nu_ref, x_ref, wg_ref, wu_ref, wd_ref, o_ref):
    del be_ref
    j = pl.program_id(0)

    @pl.when(j < nu_ref[0])
    def _():
        x = x_ref[...].astype(BF16)
        hid = _silu(_dot(x, wg_ref[0])) * _dot(x, wu_ref[0])
        o_ref[...] = _dot(hid.astype(BF16), wd_ref[0])

    @pl.when(j >= nu_ref[0])
    def _():
        o_ref[...] = jnp.zeros(o_ref.shape, F32)


def _experts(block_e, n_used, xb, wg, wu, wd):
    n_rows, d = xb.shape
    f = wg.shape[2]
    nb = n_rows // MOE_ROWS

    def blk(j, be, nu):
        return jnp.minimum(j, nu[0] - 1)

    grid_spec = pltpu.PrefetchScalarGridSpec(
        num_scalar_prefetch=2,
        grid=(nb,),
        in_specs=[
            pl.BlockSpec((MOE_ROWS, d), lambda j, be, nu: (blk(j, be, nu), 0)),
            pl.BlockSpec((1, d, f), lambda j, be, nu: (be[blk(j, be, nu)], 0, 0)),
            pl.BlockSpec((1, d, f), lambda j, be, nu: (be[blk(j, be, nu)], 0, 0)),
            pl.BlockSpec((1, f, d), lambda j, be, nu: (be[blk(j, be, nu)], 0, 0)),
        ],
        out_specs=pl.BlockSpec((MOE_ROWS, d), lambda j, be, nu: (j, 0)),
    )
    return pl.pallas_call(
        _expert_kernel,
        grid_spec=grid_spec,
        out_shape=jax.ShapeDtypeStruct((n_rows, d), F32),
        compiler_params=_cparams(("arbitrary",)),
        name="moe_experts",
    )(block_e, n_used, xb, wg, wu, wd)


def _combine_kernel(final, dest_ref, x_ref, wt_ref, g2_ref, fw_ref, yb_ref, o_ref, r0_scr, r1_scr, sem):
    tm = x_ref.shape[0]

    def start(t, carry):
        _row_copy(yb_ref, dest_ref[0, t], r0_scr, t, sem).start()
        _row_copy(yb_ref, dest_ref[1, t], r1_scr, t, sem).start()
        return carry

    lax.fori_loop(0, tm, start, 0)

    def wait(t, carry):
        _row_copy(yb_ref, 0, r0_scr, 0, sem).wait()
        _row_copy(yb_ref, 0, r1_scr, 0, sem).wait()
        return carry

    lax.fori_loop(0, tm, wait, 0)

    wt = wt_ref[...]
    ffn = wt[:, 0:1] * r0_scr[...] + wt[:, 1:2] * r1_scr[...]
    xn = x_ref[...] + g2_ref[0] * ffn
    if final:
        ms = jnp.mean(xn * xn, axis=-1, keepdims=True)
        xn = xn * lax.rsqrt(ms + EPS) * fw_ref[...]
    o_ref[...] = xn


def _combine(dest, x2, wt, g2, fw, yb, seq, final):
    n, d = x2.shape
    tm = min(TOKEN_TILE, seq)
    tpb = seq // tm
    return pl.pallas_call(
        functools.partial(_combine_kernel, final),
        grid=(n // tm,),
        in_specs=[
            pl.BlockSpec((2, tm), lambda i: (0, i), memory_space=pltpu.SMEM),
            pl.BlockSpec((tm, d), lambda i: (i, 0)),
            pl.BlockSpec((tm, 2), lambda i: (i, 0)),
            pl.BlockSpec((1, 1, d), lambda i: (i // tpb, 0, 0)),
            pl.BlockSpec((1, d), lambda i: (0, 0)),
            pl.BlockSpec(memory_space=pl.ANY),
        ],
        out_specs=pl.BlockSpec((tm, d), lambda i: (i, 0)),
        out_shape=jax.ShapeDtypeStruct((n, d), F32),
        scratch_shapes=[
            pltpu.VMEM((tm, d), F32),
            pltpu.VMEM((tm, d), F32),
            pltpu.SemaphoreType.DMA(()),
        ],
        compiler_params=_cparams(("arbitrary",)),
        name="moe_combine",
    )(dest, x2, wt, g2, fw, yb)


def _layer_params(l, d, w_in, conv_w, conv_b, dt_bias, a_log, d_skip, ssm_norm_w,
                  lb_all, hgrn_norm_w):
    gn = SSM_GROUPS * SSM_STATE
    n_heads = d // SSM_HEAD_DIM
    o_b, o_c, o_dt = 2 * d, 2 * d + gn, 2 * d + 2 * gn
    o_rest = o_dt + n_heads
    w = w_in[l]
    w_r = jnp.concatenate(
        [w[:, :2 * d], w[:, o_rest:o_rest + 6 * d], w[:, o_b:o_c], w[:, o_c:o_dt]], axis=1).astype(BF16)
    w_dt = jnp.pad(w[:, o_dt:o_rest], ((0, 0), (0, LANES - n_heads))).astype(BF16)
    pad_h = lambda v: jnp.pad(v.astype(F32), (0, LANES - n_heads)).reshape(1, LANES)
    lb = jnp.maximum(lb_all[l], 0.0)
    head_of_col = jnp.arange(d, dtype=jnp.int32) // SSM_HEAD_DIM
    head_expand = (jnp.arange(LANES, dtype=jnp.int32)[:, None] == head_of_col[None, :]).astype(BF16)
    cw, cb = conv_w[l], conv_b[l]
    return dict(
        w_r=w_r, w_dt=w_dt,
        cwx=cw[:, :d], cwb=cw[:, d:d + gn], cwc=cw[:, d + gn:],
        cbx=cb[:d].reshape(1, d), cbb=cb[d:d + gn].reshape(1, gn), cbc=cb[d + gn:].reshape(1, gn),
        dt_bias=pad_h(dt_bias[l]), a_log=pad_h(a_log[l]),
        d_skip=jnp.repeat(d_skip[l].astype(F32), SSM_HEAD_DIM).reshape(1, d),
        ssm_norm_w=ssm_norm_w[l].reshape(1, d),
        head_expand=head_expand,
        log_lb=jnp.log(lb).reshape(1, d), log_1m_lb=jnp.log1p(-lb).reshape(1, d),
        hgrn_norm_w=hgrn_norm_w[l].reshape(1, d),
    )


def kernel(x, c, ada_w, ada_b, norm1_w, norm2_w, w_in, conv_w, conv_b, dt_bias, a_log, d_skip,
           ssm_norm_w, hgrn_lower_bounds, hgrn_norm_w, w_out, router_w, router_bias,
           w_gate, w_up, w_down, final_norm_w):
    bsz, seq, d = x.shape
    n_layers = ada_w.shape[0]
    n_exp = router_w.shape[1]
    n = bsz * seq
    n_assign = 2 * n
    nb = n_assign // MOE_ROWS + n_exp
    n_rows = nb * MOE_ROWS

    lb_all = jnp.cumsum(jax.nn.softmax(hgrn_lower_bounds.astype(F32), axis=0), axis=0)
    lb_all = lb_all - lb_all[0:1]
    mod = _modulation(c, ada_w, ada_b)
    rwt = router_w.astype(F32).T
    rb = router_bias.astype(F32).reshape(n_exp, 1)
    fw = final_norm_w.reshape(1, d)

    x2 = x.reshape(n, d)
    for l in range(n_layers):
        sh1, sc1, g1, sh2, sc2, g2 = [mod[l, :, k * d:(k + 1) * d].reshape(bsz, 1, d) for k in range(6)]
        p = _layer_params(l, d, w_in, conv_w, conv_b, dt_bias, a_log, d_skip, ssm_norm_w,
                          lb_all, hgrn_norm_w)
        proj, dt_raw = _in_projection(x2, norm1_w[l].reshape(1, d), sc1, sh1, p["w_r"], p["w_dt"], seq)
        ya = _ssd_branch(proj, dt_raw, p, bsz, seq, d)
        yb = _hgrn_branch(proj, p, bsz, seq, d)
        x2, h2, mi, mw, cnt = _out_projection(
            x2, ya, yb, proj, w_out[l].astype(BF16), g1, norm2_w[l].reshape(1, d), sc2, sh2, rwt, rb, seq)

        counts = cnt[:, 0].astype(jnp.int32)
        padded = (counts + MOE_ROWS - 1) // MOE_ROWS * MOE_ROWS
        pend = jnp.cumsum(padded)
        pstart = pend - padded
        dest = jnp.take(pstart, mi[0:2]) + mi[2:4]
        block_e = jnp.minimum(
            jnp.searchsorted(pend, jnp.arange(nb, dtype=jnp.int32) * MOE_ROWS, side="right"),
            n_exp - 1).astype(jnp.int32)
        n_used = (pend[-1:] // MOE_ROWS).astype(jnp.int32)

        xb = _dispatch(dest, h2, n_rows)
        yexp = _experts(block_e, n_used, xb, w_gate[l].astype(BF16), w_up[l].astype(BF16),
                        w_down[l].astype(BF16))
        x2 = _combine(dest, x2, mw.T, g2, fw, yexp, seq, final=(l == n_layers - 1))
    return x2.reshape(bsz, seq, d)
```

```python
import functools

import jax
import jax.numpy as jnp
from jax import lax
from jax.experimental import pallas as pl
from jax.experimental.pallas import tpu as pltpu

F32 = jnp.float32
BF16 = jnp.bfloat16
EPS = 1e-6
LOG2E = 1.4426950408889634

SSM_HEAD_DIM = 64
SSM_STATE = 128
SSM_GROUPS = 4
CONV_K = 4
HGRN_HEAD = 128
N_EXPERT_GROUPS = 4
EXPERTS_PER_GROUP = 8
LANES = 128
SUBLANES = 8

SSD_CHUNK = 128
HGRN_CHUNK = 128
HGRN_HEADS_PER_STEP = 4
HGRN_CHUNKS_PER_STEP = 2
MOE_ROWS = 256
INPROJ_TM = 1024
INPROJ_TN = 1024
OUT_TM = 256
ROUTE_ROWS = 128
TOKEN_TILE = 256
DMA_UNROLL = 8
VMEM_LIMIT = 56 * 1024 * 1024


def _cparams(sem):
    return pltpu.CompilerParams(dimension_semantics=sem, vmem_limit_bytes=VMEM_LIMIT)


def _sigmoid(x):
    return 1.0 / (1.0 + jnp.exp(-x))


def _silu(x):
    return x * _sigmoid(x)


def _split3(x):
    hi = x.astype(BF16)
    r1 = x - hi.astype(F32)
    mid = r1.astype(BF16)
    lo = (r1 - mid.astype(F32)).astype(BF16)
    return hi, mid, lo


def _dot(a, b):
    return jnp.dot(a, b, preferred_element_type=F32)


def _dot_nt(a, b):
    return lax.dot_general(a, b, (((1,), (1,)), ((), ())), preferred_element_type=F32)


def _sel_dot(sel_bf16, x):
    hi, mid, lo = _split3(x)
    return _dot(sel_bf16, hi) + (_dot(sel_bf16, mid) + _dot(sel_bf16, lo))


def _dot_sel(x, sel_bf16):
    hi, mid, lo = _split3(x)
    return _dot(hi, sel_bf16) + (_dot(mid, sel_bf16) + _dot(lo, sel_bf16))


def _tril_ones(n):
    r = lax.broadcasted_iota(jnp.int32, (n, n), 0)
    c = lax.broadcasted_iota(jnp.int32, (n, n), 1)
    return jnp.where(r >= c, 1.0, 0.0).astype(BF16)


def _mod_kernel(c_ref, w_ref, b_ref, o_ref):
    ca = _silu(c_ref[...]).astype(BF16)
    o_ref[0] = _dot(ca, w_ref[0].astype(BF16)) + b_ref[0]


def _modulation(c, ada_w, ada_b):
    n_layers, d, n6 = ada_w.shape
    bsz = c.shape[0]
    tn = 1024
    return pl.pallas_call(
        _mod_kernel,
        grid=(n_layers, n6 // tn),
        in_specs=[
            pl.BlockSpec((bsz, d), lambda l, j: (0, 0)),
            pl.BlockSpec((1, d, tn), lambda l, j: (l, 0, j)),
            pl.BlockSpec((1, 1, tn), lambda l, j: (l, 0, j)),
        ],
        out_specs=pl.BlockSpec((1, bsz, tn), lambda l, j: (l, 0, j)),
        out_shape=jax.ShapeDtypeStruct((n_layers, bsz, n6), F32),
        compiler_params=_cparams(("arbitrary", "arbitrary")),
        name="adaln_mod",
    )(c, ada_w, ada_b.reshape(n_layers, 1, n6))


def _rms_mod(x, nw, sc, sh):
    ms = jnp.mean(x * x, axis=-1, keepdims=True)
    return (x * lax.rsqrt(ms + EPS) * nw) * (1.0 + sc) + sh


def _inproj_kernel(x_ref, nw_ref, sc_ref, sh_ref, w_ref, wdt_ref, o_ref, dt_ref, h_scr):
    @pl.when(pl.program_id(1) == 0)
    def _():
        h = _rms_mod(x_ref[...], nw_ref[...], sc_ref[0], sh_ref[0]).astype(BF16)
        h_scr[...] = h
        dt_ref[...] = _dot(h, wdt_ref[...])

    o_ref[...] = _dot(h_scr[...], w_ref[...])


def _in_projection(x2, nw, sc, sh, w_r, w_dt, seq):
    n, d = x2.shape
    np_ = w_r.shape[1]
    tm = min(INPROJ_TM, seq)
    tn = INPROJ_TN
    tiles_per_batch = seq // tm
    return pl.pallas_call(
        _inproj_kernel,
        grid=(n // tm, np_ // tn),
        in_specs=[
            pl.BlockSpec((tm, d), lambda i, j: (i, 0)),
            pl.BlockSpec((1, d), lambda i, j: (0, 0)),
            pl.BlockSpec((1, 1, d), lambda i, j: (i // tiles_per_batch, 0, 0)),
            pl.BlockSpec((1, 1, d), lambda i, j: (i // tiles_per_batch, 0, 0)),
            pl.BlockSpec((d, tn), lambda i, j: (0, j)),
            pl.BlockSpec((d, LANES), lambda i, j: (0, 0)),
        ],
        out_specs=[
            pl.BlockSpec((tm, tn), lambda i, j: (i, j)),
            pl.BlockSpec((tm, LANES), lambda i, j: (i, 0)),
        ],
        out_shape=[
            jax.ShapeDtypeStruct((n, np_), F32),
            jax.ShapeDtypeStruct((n, LANES), F32),
        ],
        scratch_shapes=[pltpu.VMEM((tm, d), BF16)],
        compiler_params=_cparams(("arbitrary", "arbitrary")),
        name="norm_inproj",
    )(x2, nw, sc, sh, w_r, w_dt)


def _causal_conv(ext_ref, u, w_ref, b_ref, first):
    q = u.shape[0]

    @pl.when(first)
    def _():
        ext_ref[0:SUBLANES, :] = jnp.zeros((SUBLANES, u.shape[1]), F32)

    ext_ref[SUBLANES:SUBLANES + q, :] = u
    acc = b_ref[...] + w_ref[CONV_K - 1:CONV_K, :] * u
    for k in range(CONV_K - 1):
        off = SUBLANES - (CONV_K - 1) + k
        acc = acc + w_ref[k:k + 1, :] * ext_ref[off:off + q, :]
    ext_ref[0:SUBLANES, :] = ext_ref[q:q + SUBLANES, :]
    return _silu(acc)


def _ssd_kernel(z_ref, xs_ref, bm_ref, cm_ref, dt_ref,
                cwx_ref, cwb_ref, cwc_ref, cbx_ref, cbb_ref, cbc_ref,
                dtb_ref, alog_ref, dskip_ref, nw_ref, exp_ref,
                o_ref,
                state_scr, extx_scr, extb_scr, extc_scr):
    q = z_ref.shape[0]
    d_inner = z_ref.shape[1]
    gw = d_inner // SSM_GROUPS
    heads_per_group = gw // SSM_HEAD_DIM
    first = pl.program_id(1) == 0

    @pl.when(first)
    def _():
        state_scr[...] = jnp.zeros(state_scr.shape, F32)

    xs = _causal_conv(extx_scr, xs_ref[...], cwx_ref, cbx_ref, first)
    bmat = _causal_conv(extb_scr, bm_ref[...], cwb_ref, cbb_ref, first)
    cmat = _causal_conv(extc_scr, cm_ref[...], cwc_ref, cbc_ref, first)

    x_dt = dt_ref[...] + dtb_ref[...]
    dt = jnp.maximum(x_dt, 0.0) + jnp.log(1.0 + jnp.exp(-jnp.abs(x_dt)))
    a = -jnp.exp(alog_ref[...])
    acs = _sel_dot(_tril_ones(q), dt * a)
    acs_last = acs[q - 1:q, :]
    stack = jnp.concatenate(
        [dt, jnp.exp(acs), jnp.exp(acs_last - acs),
         jnp.broadcast_to(jnp.exp(acs_last), (SUBLANES, LANES))], axis=0)
    wide = _dot_sel(stack, exp_ref[...])
    dt_e = wide[0:q]
    eacs_e = wide[q:2 * q]
    dec_e = wide[2 * q:3 * q]
    cd_e = wide[3 * q:3 * q + 1]

    xdt = xs * dt_e
    xdt_b = xdt.astype(BF16)
    xdec_b = (xdt * dec_e).astype(BF16)

    acs_t = acs.T
    ri = lax.broadcasted_iota(jnp.int32, (q, q), 0)
    ci = lax.broadcasted_iota(jnp.int32, (q, q), 1)
    causal = ri >= ci
    lane = lax.broadcasted_iota(jnp.int32, (q, LANES), 1)
    lo_half = lane < SSM_HEAD_DIM

    y_parts = []
    for g in range(SSM_GROUPS):
        bg = bmat[:, g * SSM_STATE:(g + 1) * SSM_STATE]
        cg = cmat[:, g * SSM_STATE:(g + 1) * SSM_STATE]
        bg_b = bg.astype(BF16)
        cg_b = cg.astype(BF16)
        cb = _dot_nt(cg_b, bg_b)
        st = state_scr[g]
        y_off = _dot(cg_b, st.astype(BF16)) * eacs_e[:, g * gw:(g + 1) * gw]
        diag = []
        for pair in range(heads_per_group // 2):
            ms = []
            for sub in range(2):
                h = g * heads_per_group + pair * 2 + sub
                rel = acs[:, h:h + 1] - acs_t[h:h + 1, :]
                lmat = jnp.exp(jnp.where(causal, rel, -jnp.inf))
                ms.append((cb * lmat).astype(BF16))
            lhs = jnp.concatenate(ms, axis=1)
            c0 = (g * heads_per_group + pair * 2) * SSM_HEAD_DIM
            xp = xdt_b[:, c0:c0 + LANES]
            zero = jnp.zeros_like(xp)
            rhs = jnp.concatenate(
                [jnp.where(lo_half, xp, zero), jnp.where(lo_half, zero, xp)], axis=0)
            diag.append(_dot(lhs, rhs))
        y_parts.append(jnp.concatenate(diag, axis=1) + y_off)
        new_state = st * cd_e[:, g * gw:(g + 1) * gw] + _dot(
            bg.T.astype(BF16), xdec_b[:, g * gw:(g + 1) * gw])
        state_scr[g] = new_state

    y = jnp.concatenate(y_parts, axis=1) + dskip_ref[...] * xs
    y = y * _silu(z_ref[...])
    outs = []
    for g in range(SSM_GROUPS):
        yg = y[:, g * gw:(g + 1) * gw]
        ms = jnp.mean(yg * yg, axis=-1, keepdims=True)
        outs.append(yg * lax.rsqrt(ms + EPS))
    o_ref[...] = (jnp.concatenate(outs, axis=1) * nw_ref[...]).astype(o_ref.dtype)


def _ssd_branch(proj, dt_raw, p, bsz, seq, d):
    q = min(SSD_CHUNK, seq)
    nc = seq // q
    gn = SSM_GROUPS * SSM_STATE
    b_blk = 8 * d // gn
    row = lambda b, c: b * nc + c
    vec = lambda width: pl.BlockSpec((1, width), lambda b, c: (0, 0))
    return pl.pallas_call(
        _ssd_kernel,
        grid=(bsz, nc),
        in_specs=[
            pl.BlockSpec((q, d), lambda b, c: (row(b, c), 0)),
            pl.BlockSpec((q, d), lambda b, c: (row(b, c), 1)),
            pl.BlockSpec((q, gn), lambda b, c: (row(b, c), b_blk)),
            pl.BlockSpec((q, gn), lambda b, c: (row(b, c), b_blk + 1)),
            pl.BlockSpec((q, LANES), lambda b, c: (row(b, c), 0)),
            pl.BlockSpec((CONV_K, d), lambda b, c: (0, 0)),
            pl.BlockSpec((CONV_K, gn), lambda b, c: (0, 0)),
            pl.BlockSpec((CONV_K, gn), lambda b, c: (0, 0)),
            vec(d), vec(gn), vec(gn),
            vec(LANES), vec(LANES), vec(d), vec(d),
            pl.BlockSpec((LANES, d), lambda b, c: (0, 0)),
        ],
        out_specs=pl.BlockSpec((q, d), lambda b, c: (row(b, c), 0)),
        out_shape=jax.ShapeDtypeStruct((bsz * seq, d), BF16),
        scratch_shapes=[
            pltpu.VMEM((SSM_GROUPS, SSM_STATE, d // SSM_GROUPS), F32),
            pltpu.VMEM((q + SUBLANES, d), F32),
            pltpu.VMEM((q + SUBLANES, gn), F32),
            pltpu.VMEM((q + SUBLANES, gn), F32),
        ],
        compiler_params=_cparams(("arbitrary", "arbitrary")),
        name="ssd_scan",
    )(proj, proj, proj, proj, dt_raw,
      p["cwx"], p["cwb"], p["cwc"], p["cbx"], p["cbb"], p["cbc"],
      p["dt_bias"], p["a_log"], p["d_skip"], p["ssm_norm_w"], p["head_expand"])


def _hgrn_chunk(q_raw, f_raw, v, g_raw, loglb, log1mlb, nw, s_t, tril, xor_idx, below):
    c = q_raw.shape[0]
    nv = c // SUBLANES
    logsig = jnp.minimum(f_raw, 0.0) - jnp.log(1.0 + jnp.exp(-jnp.abs(f_raw)))
    bb = log1mlb + logsig
    logf2 = (jnp.maximum(loglb, bb) + jnp.log(1.0 + jnp.exp(-jnp.abs(loglb - bb)))) * LOG2E
    kk = jnp.exp(bb - f_raw)
    qf = _silu(q_raw)
    b = _sel_dot(tril, logf2)

    b3 = b.reshape(nv, SUBLANES, LANES)
    q3 = qf.reshape(nv, SUBLANES, LANES)
    k3 = kk.reshape(nv, SUBLANES, LANES)
    sub = lax.broadcasted_iota(jnp.int32, (1, SUBLANES, LANES), 1)

    def brow(i):
        return jnp.broadcast_to(b3[:, i:i + 1, :], b3.shape)

    levels = []
    up = (sub & 1) != 0
    levels.append((1, jnp.where(up, q3 * jnp.exp2(logf2.reshape(b3.shape)), k3).reshape(c, LANES)))
    up = (sub & 2) != 0
    d = (b3 - jnp.where(sub < 4, brow(1), brow(5))) * jnp.where(up, 1.0, -1.0)
    levels.append((2, (jnp.where(up, q3, k3) * jnp.exp2(d)).reshape(c, LANES)))
    up = sub >= 4
    d = (b3 - brow(3)) * jnp.where(up, 1.0, -1.0)
    levels.append((4, (jnp.where(up, q3, k3) * jnp.exp2(d)).reshape(c, LANES)))
    w = SUBLANES
    while w < c:
        parts = []
        for j in range(c // (2 * w)):
            lo, mid, hi = 2 * w * j, 2 * w * j + w, 2 * w * (j + 1)
            bm = b[mid - 1:mid, :]
            parts.append(kk[lo:mid] * jnp.exp2(bm - b[lo:mid]))
            parts.append(qf[mid:hi] * jnp.exp2(b[mid:hi] - bm))
        levels.append((w, jnp.concatenate(parts, axis=0)))
        w *= 2

    scores = None
    for w, xw in reversed(levels):
        xb = xw.astype(BF16)
        a_w = _dot_nt(xb, xb)
        scores = a_w if scores is None else jnp.where(xor_idx < 2 * w, a_w, scores)
    scores = jnp.where(below, scores, 0.0)

    o = _dot(scores.astype(BF16), v.astype(BF16))
    o = o + jnp.sum(qf * kk, axis=-1, keepdims=True) * v
    o = o + _dot_nt((qf * jnp.exp2(b)).astype(BF16), s_t.astype(BF16))
    b_last = b[c - 1:c, :]
    kst = (kk * jnp.exp2(b_last - b)).astype(BF16)
    s_new = s_t * jnp.exp2(b_last) + _dot(v.T.astype(BF16), kst)

    ms = jnp.mean(o * o, axis=-1, keepdims=True)
    y = o * lax.rsqrt(ms + EPS) * nw * _silu(g_raw)
    return y, s_new


def _hgrn_kernel(q_ref, f_ref, i_ref, g_ref, loglb_ref, log1mlb_ref, nw_ref, o_ref, state_scr):
    rows = q_ref.shape[0]
    heads = q_ref.shape[1] // HGRN_HEAD
    c = min(HGRN_CHUNK, rows)

    @pl.when(pl.program_id(2) == 0)
    def _():
        state_scr[...] = jnp.zeros(state_scr.shape, F32)

    tril = _tril_ones(c)
    ri = lax.broadcasted_iota(jnp.int32, (c, c), 0)
    ci = lax.broadcasted_iota(jnp.int32, (c, c), 1)
    xor_idx = ri ^ ci
    below = ri > ci
    for h in range(heads):
        cols = slice(h * HGRN_HEAD, (h + 1) * HGRN_HEAD)
        s_t = state_scr[h]
        for k in range(rows // c):
            rs = slice(k * c, (k + 1) * c)
            y, s_t = _hgrn_chunk(q_ref[rs, cols], f_ref[rs, cols], i_ref[rs, cols], g_ref[rs, cols],
                                 loglb_ref[:, cols], log1mlb_ref[:, cols], nw_ref[:, cols], s_t,
                                 tril, xor_idx, below)
            o_ref[rs, cols] = y.astype(o_ref.dtype)
        state_scr[h] = s_t


def _hgrn_branch(proj, p, bsz, seq, d):
    n_heads = d // HGRN_HEAD
    hb = min(HGRN_HEADS_PER_STEP, n_heads)
    wblk = hb * HGRN_HEAD
    rows = min(HGRN_CHUNK * HGRN_CHUNKS_PER_STEP, seq)
    nt = seq // rows
    per = d // wblk

    def sect(k):
        return pl.BlockSpec((rows, wblk), lambda b, hg, t: (b * nt + t, k * per + hg))

    vec = pl.BlockSpec((1, wblk), lambda b, hg, t: (0, hg))
    return pl.pallas_call(
        _hgrn_kernel,
        grid=(bsz, n_heads // hb, nt),
        in_specs=[sect(2), sect(3), sect(4), sect(5), vec, vec, vec],
        out_specs=pl.BlockSpec((rows, wblk), lambda b, hg, t: (b * nt + t, hg)),
        out_shape=jax.ShapeDtypeStruct((bsz * seq, d), BF16),
        scratch_shapes=[pltpu.VMEM((hb, HGRN_HEAD, HGRN_HEAD), F32)],
        compiler_params=_cparams(("arbitrary", "arbitrary", "arbitrary")),
        name="hgrn2_scan",
    )(proj, proj, proj, proj, p["log_lb"], p["log_1m_lb"], p["hgrn_norm_w"])


def _first_argmax(vals, idx, n):
    m = jnp.max(vals, axis=0, keepdims=True)
    i = jnp.min(jnp.where(vals == m, idx, float(n)), axis=0, keepdims=True)
    return m, i


def _outproj_kernel(x_ref, ya_ref, yb_ref, ga_ref, gb_ref, wout_ref, g1_ref,
                    nw2_ref, sc2_ref, sh2_ref, rwt_ref, rb_ref,
                    xo_ref, h2_ref, mi_ref, mw_ref, cnt_ref, cnt_scr):
    tm = x_ref.shape[0]

    @pl.when(pl.program_id(0) == 0)
    def _():
        cnt_scr[...] = jnp.zeros(cnt_scr.shape, F32)

    r_hi, r_mid, _ = _split3(rwt_ref[...])
    ts = min(ROUTE_ROWS, tm)
    total = cnt_scr[...]
    for s in range(tm // ts):
        total = _outproj_rows(slice(s * ts, (s + 1) * ts), total, (r_hi, r_mid),
                              x_ref, ya_ref, yb_ref, ga_ref, gb_ref, wout_ref, g1_ref,
                              nw2_ref, sc2_ref, sh2_ref, rb_ref, xo_ref, h2_ref, mi_ref, mw_ref)
    cnt_scr[...] = total
    cnt_ref[...] = jnp.broadcast_to(total, cnt_ref.shape)


def _outproj_rows(rs, base, r_split, x_ref, ya_ref, yb_ref, ga_ref, gb_ref, wout_ref, g1_ref,
                  nw2_ref, sc2_ref, sh2_ref, rb_ref, xo_ref, h2_ref, mi_ref, mw_ref):
    tm = rs.stop - rs.start
    n_exp = rb_ref.shape[0]
    r_hi, r_mid = r_split
    merged = (_sigmoid(ga_ref[rs, :]) * ya_ref[rs, :].astype(F32)
              + _sigmoid(gb_ref[rs, :]) * yb_ref[rs, :].astype(F32))
    mix = _dot(merged.astype(BF16), wout_ref[...])
    xn = x_ref[rs, :] + g1_ref[0] * mix
    xo_ref[rs, :] = xn
    h2 = _rms_mod(xn, nw2_ref[...], sc2_ref[0], sh2_ref[0])
    h2_ref[rs, :] = h2

    h_hi, h_mid, _ = _split3(h2)
    logits = _dot_nt(r_hi, h_hi) + (_dot_nt(r_hi, h_mid) + _dot_nt(r_mid, h_hi))
    scores = _sigmoid(logits)
    sel = scores + rb_ref[...]

    sub = lax.broadcasted_iota(jnp.int32, (EXPERTS_PER_GROUP, tm), 0).astype(F32)
    neg = -jnp.inf
    best = None
    for g in range(N_EXPERT_GROUPS):
        blk = sel[g * EXPERTS_PER_GROUP:(g + 1) * EXPERTS_PER_GROUP]
        m1, i1 = _first_argmax(blk, sub, EXPERTS_PER_GROUP)
        m2 = jnp.max(jnp.where(sub == i1, neg, blk), axis=0, keepdims=True)
        gs = m1 + m2
        if best is None:
            best, gi = gs, jnp.zeros_like(gs)
            in_sel, in_sc = blk, scores[0:EXPERTS_PER_GROUP]
        else:
            upd = gs > best
            best = jnp.where(upd, gs, best)
            gi = jnp.where(upd, float(g), gi)
            in_sel = jnp.where(upd, blk, in_sel)
            in_sc = jnp.where(upd, scores[g * EXPERTS_PER_GROUP:(g + 1) * EXPERTS_PER_GROUP], in_sc)
    _, l1 = _first_argmax(in_sel, sub, EXPERTS_PER_GROUP)
    _, l2 = _first_argmax(jnp.where(sub == l1, neg, in_sel), sub, EXPERTS_PER_GROUP)
    s1 = jnp.sum(jnp.where(sub == l1, in_sc, 0.0), axis=0, keepdims=True)
    s2 = jnp.sum(jnp.where(sub == l2, in_sc, 0.0), axis=0, keepdims=True)
    e1 = gi * EXPERTS_PER_GROUP + l1
    e2 = gi * EXPERTS_PER_GROUP + l2
    mw_ref[0:1, rs] = s1 / (s1 + s2)
    mw_ref[1:2, rs] = s2 / (s1 + s2)

    eio = lax.broadcasted_iota(jnp.int32, (n_exp, tm), 0).astype(F32)
    oh1 = eio == e1
    oh2 = eio == e2
    onehot = jnp.where(oh1 | oh2, 1.0, 0.0)
    tr = lax.broadcasted_iota(jnp.int32, (tm, tm), 0)
    tc = lax.broadcasted_iota(jnp.int32, (tm, tm), 1)
    before = jnp.where(tr < tc, 1.0, 0.0).astype(BF16)
    pref = _dot(onehot.astype(BF16), before) + base
    r1 = jnp.sum(jnp.where(oh1, pref, 0.0), axis=0, keepdims=True)
    r2 = jnp.sum(jnp.where(oh2, pref, 0.0), axis=0, keepdims=True)
    mi_ref[0:1, rs] = e1.astype(jnp.int32)
    mi_ref[1:2, rs] = e2.astype(jnp.int32)
    mi_ref[2:3, rs] = r1.astype(jnp.int32)
    mi_ref[3:4, rs] = r2.astype(jnp.int32)
    return base + jnp.sum(onehot, axis=1, keepdims=True)


def _out_projection(x2, ya, yb, proj, w_out_b, g1, nw2, sc2, sh2, rwt, rb, seq):
    n, d = x2.shape
    n_exp = rwt.shape[0]
    tm = min(OUT_TM, seq)
    tpb = seq // tm
    bvec = pl.BlockSpec((1, 1, d), lambda i: (i // tpb, 0, 0))
    return pl.pallas_call(
        _outproj_kernel,
        grid=(n // tm,),
        in_specs=[
            pl.BlockSpec((tm, d), lambda i: (i, 0)),
            pl.BlockSpec((tm, d), lambda i: (i, 0)),
            pl.BlockSpec((tm, d), lambda i: (i, 0)),
            pl.BlockSpec((tm, d), lambda i: (i, 6)),
            pl.BlockSpec((tm, d), lambda i: (i, 7)),
            pl.BlockSpec((d, d), lambda i: (0, 0)),
            bvec,
            pl.BlockSpec((1, d), lambda i: (0, 0)),
            bvec, bvec,
            pl.BlockSpec((n_exp, d), lambda i: (0, 0)),
            pl.BlockSpec((n_exp, 1), lambda i: (0, 0)),
        ],
        out_specs=[
            pl.BlockSpec((tm, d), lambda i: (i, 0)),
            pl.BlockSpec((tm, d), lambda i: (i, 0)),
            pl.BlockSpec((4, tm), lambda i: (0, i)),
            pl.BlockSpec((2, tm), lambda i: (0, i)),
            pl.BlockSpec((n_exp, LANES), lambda i: (0, 0)),
        ],
        out_shape=[
            jax.ShapeDtypeStruct((n, d), F32),
            jax.ShapeDtypeStruct((n, d), F32),
            jax.ShapeDtypeStruct((4, n), jnp.int32),
            jax.ShapeDtypeStruct((2, n), F32),
            jax.ShapeDtypeStruct((n_exp, LANES), F32),
        ],
        scratch_shapes=[pltpu.VMEM((n_exp, 1), F32)],
        compiler_params=_cparams(("arbitrary",)),
        name="merge_outproj_router",
    )(x2, ya, yb, proj, proj, w_out_b, g1, nw2, sc2, sh2, rwt, rb)


def _row_copy(src_ref, src_row, dst_ref, dst_row, sem):
    return pltpu.make_async_copy(src_ref.at[pl.ds(src_row, 1)], dst_ref.at[pl.ds(dst_row, 1)], sem)


def _dispatch_kernel(dest_ref, h_ref, xb_in_ref, xb_ref, sem):
    del xb_in_ref
    tm = h_ref.shape[0]

    def start(t, carry):
        _row_copy(h_ref, t, xb_ref, dest_ref[0, t], sem).start()
        _row_copy(h_ref, t, xb_ref, dest_ref[1, t], sem).start()
        return carry

    lax.fori_loop(0, tm, start, 0, unroll=DMA_UNROLL)
    for _ in range(2):
        pltpu.make_async_copy(h_ref, xb_ref.at[pl.ds(0, tm)], sem).wait()


def _dispatch(dest, h2, n_rows):
    n, d = h2.shape
    tm = min(TOKEN_TILE, n)
    xb0 = jnp.zeros((n_rows, d), F32)
    return pl.pallas_call(
        _dispatch_kernel,
        grid=(n // tm,),
        in_specs=[
            pl.BlockSpec((2, tm), lambda i: (0, i), memory_space=pltpu.SMEM),
            pl.BlockSpec((tm, d), lambda i: (i, 0)),
            pl.BlockSpec(memory_space=pl.ANY),
        ],
        out_specs=pl.BlockSpec(memory_space=pl.ANY),
        out_shape=jax.ShapeDtypeStruct((n_rows, d), F32),
        scratch_shapes=[pltpu.SemaphoreType.DMA(())],
        input_output_aliases={2: 0},
        compiler_params=_cparams(("arbitrary",)),
        name="moe_dispatch",
    )(dest, h2, xb0)


def _expert_kernel(be_ref, nu_ref, x_ref, wg_ref, wu_ref, wd_ref, o_ref):
    del be_ref
    j = pl.program_id(0)

    @pl.when(j < nu_ref[0])
    def _():
        x = x_ref[...].astype(BF16)
        hid = _silu(_dot(x, wg_ref[0])) * _dot(x, wu_ref[0])
        o_ref[...] = _dot(hid.astype(BF16), wd_ref[0])

    @pl.when(j >= nu_ref[0])
    def _():
        o_ref[...] = jnp.zeros(o_ref.shape, F32)


def _experts(block_e, n_used, xb, wg, wu, wd):
    n_rows, d = xb.shape
    f = wg.shape[2]
    nb = n_rows // MOE_ROWS

    def blk(j, be, nu):
        return jnp.minimum(j, nu[0] - 1)

    grid_spec = pltpu.PrefetchScalarGridSpec(
        num_scalar_prefetch=2,
        grid=(nb,),
        in_specs=[
            pl.BlockSpec((MOE_ROWS, d), lambda j, be, nu: (blk(j, be, nu), 0)),
            pl.BlockSpec((1, d, f), lambda j, be, nu: (be[blk(j, be, nu)], 0, 0)),
            pl.BlockSpec((1, d, f), lambda j, be, nu: (be[blk(j, be, nu)], 0, 0)),
            pl.BlockSpec((1, f, d), lambda j, be, nu: (be[blk(j, be, nu)], 0, 0)),
        ],
        out_specs=pl.BlockSpec((MOE_ROWS, d), lambda j, be, nu: (j, 0)),
    )
    return pl.pallas_call(
        _expert_kernel,
        grid_spec=grid_spec,
        out_shape=jax.ShapeDtypeStruct((n_rows, d), F32),
        compiler_params=_cparams(("arbitrary",)),
        name="moe_experts",
    )(block_e, n_used, xb, wg, wu, wd)


def _combine_kernel(final, dest_ref, x_ref, wt_ref, g2_ref, fw_ref, yb_ref, o_ref, r0_scr, r1_scr, sem):
    tm = x_ref.shape[0]

    def start(t, carry):
        _row_copy(yb_ref, dest_ref[0, t], r0_scr, t, sem).start()
        _row_copy(yb_ref, dest_ref[1, t], r1_scr, t, sem).start()
        return carry

    lax.fori_loop(0, tm, start, 0, unroll=DMA_UNROLL)
    pltpu.make_async_copy(yb_ref.at[pl.ds(0, tm)], r0_scr, sem).wait()
    pltpu.make_async_copy(yb_ref.at[pl.ds(0, tm)], r1_scr, sem).wait()

    wt = wt_ref[...]
    ffn = wt[:, 0:1] * r0_scr[...] + wt[:, 1:2] * r1_scr[...]
    xn = x_ref[...] + g2_ref[0] * ffn
    if final:
        ms = jnp.mean(xn * xn, axis=-1, keepdims=True)
        xn = xn * lax.rsqrt(ms + EPS) * fw_ref[...]
    o_ref[...] = xn


def _combine(dest, x2, wt, g2, fw, yb, seq, final):
    n, d = x2.shape
    tm = min(TOKEN_TILE, seq)
    tpb = seq // tm
    return pl.pallas_call(
        functools.partial(_combine_kernel, final),
        grid=(n // tm,),
        in_specs=[
            pl.BlockSpec((2, tm), lambda i: (0, i), memory_space=pltpu.SMEM),
            pl.BlockSpec((tm, d), lambda i: (i, 0)),
            pl.BlockSpec((tm, 2), lambda i: (i, 0)),
            pl.BlockSpec((1, 1, d), lambda i: (i // tpb, 0, 0)),
            pl.BlockSpec((1, d), lambda i: (0, 0)),
            pl.BlockSpec(memory_space=pl.ANY),
        ],
        out_specs=pl.BlockSpec((tm, d), lambda i: (i, 0)),
        out_shape=jax.ShapeDtypeStruct((n, d), F32),
        scratch_shapes=[
            pltpu.VMEM((tm, d), F32),
            pltpu.VMEM((tm, d), F32),
            pltpu.SemaphoreType.DMA(()),
        ],
        compiler_params=_cparams(("arbitrary",)),
        name="moe_combine",
    )(dest, x2, wt, g2, fw, yb)


def _layer_params(l, d, w_in, conv_w, conv_b, dt_bias, a_log, d_skip, ssm_norm_w,
                  lb_all, hgrn_norm_w):
    gn = SSM_GROUPS * SSM_STATE
    n_heads = d // SSM_HEAD_DIM
    o_b, o_c, o_dt = 2 * d, 2 * d + gn, 2 * d + 2 * gn
    o_rest = o_dt + n_heads
    w = w_in[l]
    w_r = jnp.concatenate(
        [w[:, :2 * d], w[:, o_rest:o_rest + 6 * d], w[:, o_b:o_c], w[:, o_c:o_dt]], axis=1).astype(BF16)
    w_dt = jnp.pad(w[:, o_dt:o_rest], ((0, 0), (0, LANES - n_heads))).astype(BF16)
    pad_h = lambda v: jnp.pad(v.astype(F32), (0, LANES - n_heads)).reshape(1, LANES)
    lb = jnp.maximum(lb_all[l], 0.0)
    head_of_col = jnp.arange(d, dtype=jnp.int32) // SSM_HEAD_DIM
    head_expand = (jnp.arange(LANES, dtype=jnp.int32)[:, None] == head_of_col[None, :]).astype(BF16)
    cw, cb = conv_w[l], conv_b[l]
    return dict(
        w_r=w_r, w_dt=w_dt,
        cwx=cw[:, :d], cwb=cw[:, d:d + gn], cwc=cw[:, d + gn:],
        cbx=cb[:d].reshape(1, d), cbb=cb[d:d + gn].reshape(1, gn), cbc=cb[d + gn:].reshape(1, gn),
        dt_bias=pad_h(dt_bias[l]), a_log=pad_h(a_log[l]),
        d_skip=jnp.repeat(d_skip[l].astype(F32), SSM_HEAD_DIM).reshape(1, d),
        ssm_norm_w=ssm_norm_w[l].reshape(1, d),
        head_expand=head_expand,
        log_lb=jnp.log(lb).reshape(1, d), log_1m_lb=jnp.log1p(-lb).reshape(1, d),
        hgrn_norm_w=hgrn_norm_w[l].reshape(1, d),
    )


def kernel(x, c, ada_w, ada_b, norm1_w, norm2_w, w_in, conv_w, conv_b, dt_bias, a_log, d_skip,
           ssm_norm_w, hgrn_lower_bounds, hgrn_norm_w, w_out, router_w, router_bias,
           w_gate, w_up, w_down, final_norm_w):
    bsz, seq, d = x.shape
    n_layers = ada_w.shape[0]
    n_exp = router_w.shape[1]
    n = bsz * seq
    n_assign = 2 * n
    nb = n_assign // MOE_ROWS + n_exp
    n_rows = nb * MOE_ROWS

    lb_all = jnp.cumsum(jax.nn.softmax(hgrn_lower_bounds.astype(F32), axis=0), axis=0)
    lb_all = lb_all - lb_all[0:1]
    mod = _modulation(c, ada_w, ada_b)
    rwt = router_w.astype(F32).T
    rb = router_bias.astype(F32).reshape(n_exp, 1)
    fw = final_norm_w.reshape(1, d)

    x2 = x.reshape(n, d)
    for l in range(n_layers):
        sh1, sc1, g1, sh2, sc2, g2 = [mod[l, :, k * d:(k + 1) * d].reshape(bsz, 1, d) for k in range(6)]
        p = _layer_params(l, d, w_in, conv_w, conv_b, dt_bias, a_log, d_skip, ssm_norm_w,
                          lb_all, hgrn_norm_w)
        proj, dt_raw = _in_projection(x2, norm1_w[l].reshape(1, d), sc1, sh1, p["w_r"], p["w_dt"], seq)
        ya = _ssd_branch(proj, dt_raw, p, bsz, seq, d)
        yb = _hgrn_branch(proj, p, bsz, seq, d)
        x2, h2, mi, mw, cnt = _out_projection(
            x2, ya, yb, proj, w_out[l].astype(BF16), g1, norm2_w[l].reshape(1, d), sc2, sh2, rwt, rb, seq)

        counts = cnt[:, 0].astype(jnp.int32)
        padded = (counts + MOE_ROWS - 1) // MOE_ROWS * MOE_ROWS
        pend = jnp.cumsum(padded)
        pstart = pend - padded
        is_e = mi[0:2, :, None] == jnp.arange(n_exp, dtype=jnp.int32)
        dest = jnp.sum(jnp.where(is_e, pstart, 0), axis=-1) + mi[2:4]
        block_row0 = jnp.arange(nb, dtype=jnp.int32) * MOE_ROWS
        block_e = jnp.minimum(
            jnp.sum((pend[None, :] <= block_row0[:, None]).astype(jnp.int32), axis=1), n_exp - 1)
        n_used = (pend[-1:] // MOE_ROWS).astype(jnp.int32)

        xb = _dispatch(dest, h2, n_rows)
        yexp = _experts(block_e, n_used, xb, w_gate[l].astype(BF16), w_up[l].astype(BF16),
                        w_down[l].astype(BF16))
        x2 = _combine(dest, x2, mw.T, g2, fw, yexp, seq, final=(l == n_layers - 1))
    return x2.reshape(bsz, seq, d)
```

```python
import functools

import jax
import jax.numpy as jnp
from jax import lax
from jax.experimental import pallas as pl
from jax.experimental.pallas import tpu as pltpu

F32 = jnp.float32
BF16 = jnp.bfloat16
EPS = 1e-6
LOG2E = 1.4426950408889634

SSM_HEAD_DIM = 64
SSM_STATE = 128
SSM_GROUPS = 4
CONV_K = 4
HGRN_HEAD = 128
N_EXPERT_GROUPS = 4
EXPERTS_PER_GROUP = 8
LANES = 128
SUBLANES = 8

SSD_CHUNK = 128
HGRN_CHUNK = 128
HGRN_HEADS_PER_STEP = 8
HGRN_CHUNKS_PER_STEP = 2
MOE_ROWS = 512
INPROJ_TM = 1024
INPROJ_TN = 1024
OUT_TM = 256
ROUTE_ROWS = 128
TOKEN_TILE = 256
DMA_UNROLL = 8
VMEM_LIMIT = 56 * 1024 * 1024


def _cparams(sem):
    return pltpu.CompilerParams(dimension_semantics=sem, vmem_limit_bytes=VMEM_LIMIT)


def _sigmoid(x):
    return 1.0 / (1.0 + jnp.exp(-x))


def _silu(x):
    return x * _sigmoid(x)


def _split3(x):
    hi = x.astype(BF16)
    r1 = x - hi.astype(F32)
    mid = r1.astype(BF16)
    lo = (r1 - mid.astype(F32)).astype(BF16)
    return hi, mid, lo


def _dot(a, b):
    return jnp.dot(a, b, preferred_element_type=F32)


def _dot_nt(a, b):
    return lax.dot_general(a, b, (((1,), (1,)), ((), ())), preferred_element_type=F32)


def _sel_dot(sel_bf16, x):
    hi, mid, lo = _split3(x)
    return _dot(sel_bf16, hi) + (_dot(sel_bf16, mid) + _dot(sel_bf16, lo))


def _dot_sel(x, sel_bf16):
    hi, mid, lo = _split3(x)
    return _dot(hi, sel_bf16) + (_dot(mid, sel_bf16) + _dot(lo, sel_bf16))


def _tril_ones(n):
    r = lax.broadcasted_iota(jnp.int32, (n, n), 0)
    c = lax.broadcasted_iota(jnp.int32, (n, n), 1)
    return jnp.where(r >= c, 1.0, 0.0).astype(BF16)


def _mod_kernel(c_ref, w_ref, b_ref, o_ref):
    ca = _silu(c_ref[...]).astype(BF16)
    o_ref[0] = _dot(ca, w_ref[0].astype(BF16)) + b_ref[0]


def _modulation(c, ada_w, ada_b):
    n_layers, d, n6 = ada_w.shape
    bsz = c.shape[0]
    tn = 1024
    return pl.pallas_call(
        _mod_kernel,
        grid=(n_layers, n6 // tn),
        in_specs=[
            pl.BlockSpec((bsz, d), lambda l, j: (0, 0)),
            pl.BlockSpec((1, d, tn), lambda l, j: (l, 0, j)),
            pl.BlockSpec((1, 1, tn), lambda l, j: (l, 0, j)),
        ],
        out_specs=pl.BlockSpec((1, bsz, tn), lambda l, j: (l, 0, j)),
        out_shape=jax.ShapeDtypeStruct((n_layers, bsz, n6), F32),
        compiler_params=_cparams(("arbitrary", "arbitrary")),
        name="adaln_mod",
    )(c, ada_w, ada_b.reshape(n_layers, 1, n6))


def _relayout_kernel(shift, src_ref, shifted_ref, wa_ref, wb_ref, o_ref):
    del src_ref
    t = pl.program_id(0)

    @pl.when(shifted_ref[t] == 0)
    def _():
        o_ref[...] = wa_ref[...].astype(BF16)

    @pl.when(shifted_ref[t] != 0)
    def _():
        o_ref[...] = jnp.concatenate([wa_ref[:, shift:], wb_ref[:, :shift]], axis=1).astype(BF16)


def _relayout_w_in(w, d):
    gn2 = 2 * SSM_GROUPS * SSM_STATE
    n_heads = d // SSM_HEAD_DIM
    tn = INPROJ_TN
    assert gn2 == tn and (2 * d) % tn == 0 and 0 < n_heads < LANES
    n_front, n_rest = 2 * d // tn, 6 * d // tn
    src = list(range(n_front)) + [n_front + 1 + k for k in range(n_rest)] + [n_front]
    shifted = [0] * n_front + [1] * n_rest + [0]
    n_tiles = len(src)
    grid_spec = pltpu.PrefetchScalarGridSpec(
        num_scalar_prefetch=2,
        grid=(n_tiles,),
        in_specs=[
            pl.BlockSpec((d, tn), lambda t, s, f: (0, s[t])),
            pl.BlockSpec((d, LANES), lambda t, s, f: (0, (s[t] + 1) * (tn // LANES))),
        ],
        out_specs=pl.BlockSpec((d, tn), lambda t, s, f: (0, t)),
    )
    return pl.pallas_call(
        functools.partial(_relayout_kernel, n_heads),
        grid_spec=grid_spec,
        out_shape=jax.ShapeDtypeStruct((d, n_tiles * tn), BF16),
        compiler_params=_cparams(("arbitrary",)),
        name="w_in_relayout",
    )(jnp.asarray(src, jnp.int32), jnp.asarray(shifted, jnp.int32), w, w)


def _rms_mod(x, nw, sc, sh):
    ms = jnp.mean(x * x, axis=-1, keepdims=True)
    return (x * lax.rsqrt(ms + EPS) * nw) * (1.0 + sc) + sh


def _inproj_kernel(x_ref, nw_ref, sc_ref, sh_ref, w_ref, wdt_ref, o_ref, dt_ref, h_scr):
    @pl.when(pl.program_id(1) == 0)
    def _():
        h = _rms_mod(x_ref[...], nw_ref[...], sc_ref[0], sh_ref[0]).astype(BF16)
        h_scr[...] = h
        dt_ref[...] = _dot(h, wdt_ref[...])

    o_ref[...] = _dot(h_scr[...], w_ref[...])


def _in_projection(x2, nw, sc, sh, w_r, w_dt, seq):
    n, d = x2.shape
    np_ = w_r.shape[1]
    tm = min(INPROJ_TM, seq)
    tn = INPROJ_TN
    tiles_per_batch = seq // tm
    return pl.pallas_call(
        _inproj_kernel,
        grid=(n // tm, np_ // tn),
        in_specs=[
            pl.BlockSpec((tm, d), lambda i, j: (i, 0)),
            pl.BlockSpec((1, d), lambda i, j: (0, 0)),
            pl.BlockSpec((1, 1, d), lambda i, j: (i // tiles_per_batch, 0, 0)),
            pl.BlockSpec((1, 1, d), lambda i, j: (i // tiles_per_batch, 0, 0)),
            pl.BlockSpec((d, tn), lambda i, j: (0, j)),
            pl.BlockSpec((d, LANES), lambda i, j: (0, 0)),
        ],
        out_specs=[
            pl.BlockSpec((tm, tn), lambda i, j: (i, j)),
            pl.BlockSpec((tm, LANES), lambda i, j: (i, 0)),
        ],
        out_shape=[
            jax.ShapeDtypeStruct((n, np_), F32),
            jax.ShapeDtypeStruct((n, LANES), F32),
        ],
        scratch_shapes=[pltpu.VMEM((tm, d), BF16)],
        compiler_params=_cparams(("arbitrary", "arbitrary")),
        name="norm_inproj",
    )(x2, nw, sc, sh, w_r, w_dt)


def _causal_conv(ext_ref, u, w_ref, b_ref, first):
    q = u.shape[0]

    @pl.when(first)
    def _():
        ext_ref[0:SUBLANES, :] = jnp.zeros((SUBLANES, u.shape[1]), F32)

    ext_ref[SUBLANES:SUBLANES + q, :] = u
    acc = b_ref[...] + w_ref[CONV_K - 1:CONV_K, :] * u
    for k in range(CONV_K - 1):
        off = SUBLANES - (CONV_K - 1) + k
        acc = acc + w_ref[k:k + 1, :] * ext_ref[off:off + q, :]
    ext_ref[0:SUBLANES, :] = ext_ref[q:q + SUBLANES, :]
    return _silu(acc)


def _ssd_kernel(z_ref, xs_ref, bm_ref, cm_ref, dt_ref,
                cwx_ref, cwb_ref, cwc_ref, cbx_ref, cbb_ref, cbc_ref,
                dtb_ref, alog_ref, dskip_ref, nw_ref, exp_ref,
                o_ref,
                state_scr, extx_scr, extb_scr, extc_scr):
    q = z_ref.shape[0]
    d_inner = z_ref.shape[1]
    gw = d_inner // SSM_GROUPS
    heads_per_group = gw // SSM_HEAD_DIM
    first = pl.program_id(1) == 0

    @pl.when(first)
    def _():
        state_scr[...] = jnp.zeros(state_scr.shape, F32)

    xs = _causal_conv(extx_scr, xs_ref[...], cwx_ref, cbx_ref, first)
    bmat = _causal_conv(extb_scr, bm_ref[...], cwb_ref, cbb_ref, first)
    cmat = _causal_conv(extc_scr, cm_ref[...], cwc_ref, cbc_ref, first)

    x_dt = dt_ref[...] + dtb_ref[...]
    dt = jnp.maximum(x_dt, 0.0) + jnp.log(1.0 + jnp.exp(-jnp.abs(x_dt)))
    a = -jnp.exp(alog_ref[...])
    acs = _sel_dot(_tril_ones(q), dt * a)
    acs_last = acs[q - 1:q, :]
    stack = jnp.concatenate(
        [dt, jnp.exp(acs), jnp.exp(acs_last - acs),
         jnp.broadcast_to(jnp.exp(acs_last), (SUBLANES, LANES))], axis=0)
    wide = _dot_sel(stack, exp_ref[...])
    dt_e = wide[0:q]
    eacs_e = wide[q:2 * q]
    dec_e = wide[2 * q:3 * q]
    cd_e = wide[3 * q:3 * q + 1]

    xdt = xs * dt_e
    xdt_b = xdt.astype(BF16)
    xdec_b = (xdt * dec_e).astype(BF16)

    acs_t = acs.T
    ri = lax.broadcasted_iota(jnp.int32, (q, q), 0)
    ci = lax.broadcasted_iota(jnp.int32, (q, q), 1)
    causal = ri >= ci
    lane = lax.broadcasted_iota(jnp.int32, (q, LANES), 1)
    lo_half = lane < SSM_HEAD_DIM

    y_parts = []
    for g in range(SSM_GROUPS):
        bg = bmat[:, g * SSM_STATE:(g + 1) * SSM_STATE]
        cg = cmat[:, g * SSM_STATE:(g + 1) * SSM_STATE]
        bg_b = bg.astype(BF16)
        cg_b = cg.astype(BF16)
        cb = _dot_nt(cg_b, bg_b)
        st = state_scr[g]
        y_off = _dot(cg_b, st.astype(BF16)) * eacs_e[:, g * gw:(g + 1) * gw]
        diag = []
        for pair in range(heads_per_group // 2):
            ms = []
            for sub in range(2):
                h = g * heads_per_group + pair * 2 + sub
                rel = acs[:, h:h + 1] - acs_t[h:h + 1, :]
                lmat = jnp.exp(jnp.where(causal, rel, -jnp.inf))
                ms.append((cb * lmat).astype(BF16))
            lhs = jnp.concatenate(ms, axis=1)
            c0 = (g * heads_per_group + pair * 2) * SSM_HEAD_DIM
            xp = xdt_b[:, c0:c0 + LANES]
            zero = jnp.zeros_like(xp)
            rhs = jnp.concatenate(
                [jnp.where(lo_half, xp, zero), jnp.where(lo_half, zero, xp)], axis=0)
            diag.append(_dot(lhs, rhs))
        y_parts.append(jnp.concatenate(diag, axis=1) + y_off)
        new_state = st * cd_e[:, g * gw:(g + 1) * gw] + _dot(
            bg.T.astype(BF16), xdec_b[:, g * gw:(g + 1) * gw])
        state_scr[g] = new_state

    y = jnp.concatenate(y_parts, axis=1) + dskip_ref[...] * xs
    y = y * _silu(z_ref[...])
    outs = []
    for g in range(SSM_GROUPS):
        yg = y[:, g * gw:(g + 1) * gw]
        ms = jnp.mean(yg * yg, axis=-1, keepdims=True)
        outs.append(yg * lax.rsqrt(ms + EPS))
    o_ref[...] = (jnp.concatenate(outs, axis=1) * nw_ref[...]).astype(o_ref.dtype)


def _ssd_branch(proj, dt_raw, p, bsz, seq, d):
    q = min(SSD_CHUNK, seq)
    nc = seq // q
    gn = SSM_GROUPS * SSM_STATE
    b_blk = 8 * d // gn
    row = lambda b, c: b * nc + c
    vec = lambda width: pl.BlockSpec((1, width), lambda b, c: (0, 0))
    return pl.pallas_call(
        _ssd_kernel,
        grid=(bsz, nc),
        in_specs=[
            pl.BlockSpec((q, d), lambda b, c: (row(b, c), 0)),
            pl.BlockSpec((q, d), lambda b, c: (row(b, c), 1)),
            pl.BlockSpec((q, gn), lambda b, c: (row(b, c), b_blk)),
            pl.BlockSpec((q, gn), lambda b, c: (row(b, c), b_blk + 1)),
            pl.BlockSpec((q, LANES), lambda b, c: (row(b, c), 0)),
            pl.BlockSpec((CONV_K, d), lambda b, c: (0, 0)),
            pl.BlockSpec((CONV_K, gn), lambda b, c: (0, 0)),
            pl.BlockSpec((CONV_K, gn), lambda b, c: (0, 0)),
            vec(d), vec(gn), vec(gn),
            vec(LANES), vec(LANES), vec(d), vec(d),
            pl.BlockSpec((LANES, d), lambda b, c: (0, 0)),
        ],
        out_specs=pl.BlockSpec((q, d), lambda b, c: (row(b, c), 0)),
        out_shape=jax.ShapeDtypeStruct((bsz * seq, d), BF16),
        scratch_shapes=[
            pltpu.VMEM((SSM_GROUPS, SSM_STATE, d // SSM_GROUPS), F32),
            pltpu.VMEM((q + SUBLANES, d), F32),
            pltpu.VMEM((q + SUBLANES, gn), F32),
            pltpu.VMEM((q + SUBLANES, gn), F32),
        ],
        compiler_params=_cparams(("arbitrary", "arbitrary")),
        name="ssd_scan",
    )(proj, proj, proj, proj, dt_raw,
      p["cwx"], p["cwb"], p["cwc"], p["cbx"], p["cbb"], p["cbc"],
      p["dt_bias"], p["a_log"], p["d_skip"], p["ssm_norm_w"], p["head_expand"])


def _hgrn_chunk(q_raw, f_raw, v, g_raw, loglb, log1mlb, nw, s_t, tril, xor_idx, below):
    c = q_raw.shape[0]
    nv = c // SUBLANES
    f2 = f_raw * LOG2E
    logsig = jnp.minimum(f2, 0.0) - jnp.log2(1.0 + jnp.exp2(-jnp.abs(f2)))
    bb = log1mlb + logsig
    logf2 = jnp.maximum(loglb, bb) + jnp.log2(1.0 + jnp.exp2(-jnp.abs(loglb - bb)))
    kk = jnp.exp2(bb - f2)
    qf = _silu(q_raw)
    b = _sel_dot(tril, logf2)

    b3 = b.reshape(nv, SUBLANES, LANES)
    q3 = qf.reshape(nv, SUBLANES, LANES)
    k3 = kk.reshape(nv, SUBLANES, LANES)
    sub = lax.broadcasted_iota(jnp.int32, (1, SUBLANES, LANES), 1)

    def brow(i):
        return jnp.broadcast_to(b3[:, i:i + 1, :], b3.shape)

    levels = []
    up = (sub & 1) != 0
    levels.append((1, jnp.where(up, q3 * jnp.exp2(logf2.reshape(b3.shape)), k3).reshape(c, LANES)))
    up = (sub & 2) != 0
    d = (b3 - jnp.where(sub < 4, brow(1), brow(5))) * jnp.where(up, 1.0, -1.0)
    levels.append((2, (jnp.where(up, q3, k3) * jnp.exp2(d)).reshape(c, LANES)))
    up = sub >= 4
    d = (b3 - brow(3)) * jnp.where(up, 1.0, -1.0)
    levels.append((4, (jnp.where(up, q3, k3) * jnp.exp2(d)).reshape(c, LANES)))
    w = SUBLANES
    while w < c:
        parts = []
        for j in range(c // (2 * w)):
            lo, mid, hi = 2 * w * j, 2 * w * j + w, 2 * w * (j + 1)
            bm = b[mid - 1:mid, :]
            parts.append(kk[lo:mid] * jnp.exp2(bm - b[lo:mid]))
            parts.append(qf[mid:hi] * jnp.exp2(b[mid:hi] - bm))
        levels.append((w, jnp.concatenate(parts, axis=0)))
        w *= 2

    scores = None
    for w, xw in reversed(levels):
        xb = xw.astype(BF16)
        a_w = _dot_nt(xb, xb)
        scores = a_w if scores is None else jnp.where(xor_idx < 2 * w, a_w, scores)
    scores = jnp.where(below, scores, 0.0)

    o = _dot(scores.astype(BF16), v.astype(BF16))
    o = o + jnp.sum(qf * kk, axis=-1, keepdims=True) * v
    o = o + _dot_nt((qf * jnp.exp2(b)).astype(BF16), s_t.astype(BF16))
    b_last = b[c - 1:c, :]
    kst = (kk * jnp.exp2(b_last - b)).astype(BF16)
    s_new = s_t * jnp.exp2(b_last) + _dot(v.T.astype(BF16), kst)

    ms = jnp.mean(o * o, axis=-1, keepdims=True)
    y = o * lax.rsqrt(ms + EPS) * nw * _silu(g_raw)
    return y, s_new


def _hgrn_kernel(q_ref, f_ref, i_ref, g_ref, loglb_ref, log1mlb_ref, nw_ref, o_ref, state_scr):
    rows = q_ref.shape[0]
    heads = q_ref.shape[1] // HGRN_HEAD
    c = min(HGRN_CHUNK, rows)

    @pl.when(pl.program_id(2) == 0)
    def _():
        state_scr[...] = jnp.zeros(state_scr.shape, F32)

    tril = _tril_ones(c)
    ri = lax.broadcasted_iota(jnp.int32, (c, c), 0)
    ci = lax.broadcasted_iota(jnp.int32, (c, c), 1)
    xor_idx = ri ^ ci
    below = ri > ci
    for h in range(heads):
        cols = slice(h * HGRN_HEAD, (h + 1) * HGRN_HEAD)
        s_t = state_scr[h]
        for k in range(rows // c):
            rs = slice(k * c, (k + 1) * c)
            y, s_t = _hgrn_chunk(q_ref[rs, cols], f_ref[rs, cols], i_ref[rs, cols], g_ref[rs, cols],
                                 loglb_ref[:, cols], log1mlb_ref[:, cols], nw_ref[:, cols], s_t,
                                 tril, xor_idx, below)
            o_ref[rs, cols] = y.astype(o_ref.dtype)
        state_scr[h] = s_t


def _hgrn_branch(proj, p, bsz, seq, d):
    n_heads = d // HGRN_HEAD
    hb = min(HGRN_HEADS_PER_STEP, n_heads)
    wblk = hb * HGRN_HEAD
    rows = min(HGRN_CHUNK * HGRN_CHUNKS_PER_STEP, seq)
    nt = seq // rows
    per = d // wblk

    def sect(k):
        return pl.BlockSpec((rows, wblk), lambda b, hg, t: (b * nt + t, k * per + hg))

    vec = pl.BlockSpec((1, wblk), lambda b, hg, t: (0, hg))
    return pl.pallas_call(
        _hgrn_kernel,
        grid=(bsz, n_heads // hb, nt),
        in_specs=[sect(2), sect(3), sect(4), sect(5), vec, vec, vec],
        out_specs=pl.BlockSpec((rows, wblk), lambda b, hg, t: (b * nt + t, hg)),
        out_shape=jax.ShapeDtypeStruct((bsz * seq, d), BF16),
        scratch_shapes=[pltpu.VMEM((hb, HGRN_HEAD, HGRN_HEAD), F32)],
        compiler_params=_cparams(("arbitrary", "arbitrary", "arbitrary")),
        name="hgrn2_scan",
    )(proj, proj, proj, proj, p["log_lb"], p["log_1m_lb"], p["hgrn_norm_w"])


def _first_argmax(vals, idx, n):
    m = jnp.max(vals, axis=0, keepdims=True)
    i = jnp.min(jnp.where(vals == m, idx, float(n)), axis=0, keepdims=True)
    return m, i


def _outproj_kernel(x_ref, ya_ref, yb_ref, ga_ref, gb_ref, wout_ref, g1_ref,
                    nw2_ref, sc2_ref, sh2_ref, rwt_ref, rb_ref,
                    xo_ref, h2_ref, mi_ref, mw_ref, cnt_ref, cnt_scr):
    tm = x_ref.shape[0]

    @pl.when(pl.program_id(0) == 0)
    def _():
        cnt_scr[...] = jnp.zeros(cnt_scr.shape, F32)

    r_hi, r_mid, _ = _split3(rwt_ref[...])
    ts = min(ROUTE_ROWS, tm)
    total = cnt_scr[...]
    for s in range(tm // ts):
        total = _outproj_rows(slice(s * ts, (s + 1) * ts), total, (r_hi, r_mid),
                              x_ref, ya_ref, yb_ref, ga_ref, gb_ref, wout_ref, g1_ref,
                              nw2_ref, sc2_ref, sh2_ref, rb_ref, xo_ref, h2_ref, mi_ref, mw_ref)
    cnt_scr[...] = total
    cnt_ref[...] = jnp.broadcast_to(total, cnt_ref.shape)


def _outproj_rows(rs, base, r_split, x_ref, ya_ref, yb_ref, ga_ref, gb_ref, wout_ref, g1_ref,
                  nw2_ref, sc2_ref, sh2_ref, rb_ref, xo_ref, h2_ref, mi_ref, mw_ref):
    tm = rs.stop - rs.start
    n_exp = rb_ref.shape[0]
    r_hi, r_mid = r_split
    merged = (_sigmoid(ga_ref[rs, :]) * ya_ref[rs, :].astype(F32)
              + _sigmoid(gb_ref[rs, :]) * yb_ref[rs, :].astype(F32))
    mix = _dot(merged.astype(BF16), wout_ref[...])
    xn = x_ref[rs, :] + g1_ref[0] * mix
    xo_ref[rs, :] = xn
    h2 = _rms_mod(xn, nw2_ref[...], sc2_ref[0], sh2_ref[0])
    h2_ref[rs, :] = h2

    h_hi, h_mid, _ = _split3(h2)
    logits = _dot_nt(r_hi, h_hi) + (_dot_nt(r_hi, h_mid) + _dot_nt(r_mid, h_hi))
    scores = _sigmoid(logits)
    sel = scores + rb_ref[...]

    sub = lax.broadcasted_iota(jnp.int32, (EXPERTS_PER_GROUP, tm), 0).astype(F32)
    neg = -jnp.inf
    best = None
    for g in range(N_EXPERT_GROUPS):
        blk = sel[g * EXPERTS_PER_GROUP:(g + 1) * EXPERTS_PER_GROUP]
        m1, i1 = _first_argmax(blk, sub, EXPERTS_PER_GROUP)
        m2 = jnp.max(jnp.where(sub == i1, neg, blk), axis=0, keepdims=True)
        gs = m1 + m2
        if best is None:
            best, gi = gs, jnp.zeros_like(gs)
            in_sel, in_sc = blk, scores[0:EXPERTS_PER_GROUP]
        else:
            upd = gs > best
            best = jnp.where(upd, gs, best)
            gi = jnp.where(upd, float(g), gi)
            in_sel = jnp.where(upd, blk, in_sel)
            in_sc = jnp.where(upd, scores[g * EXPERTS_PER_GROUP:(g + 1) * EXPERTS_PER_GROUP], in_sc)
    _, l1 = _first_argmax(in_sel, sub, EXPERTS_PER_GROUP)
    _, l2 = _first_argmax(jnp.where(sub == l1, neg, in_sel), sub, EXPERTS_PER_GROUP)
    s1 = jnp.sum(jnp.where(sub == l1, in_sc, 0.0), axis=0, keepdims=True)
    s2 = jnp.sum(jnp.where(sub == l2, in_sc, 0.0), axis=0, keepdims=True)
    e1 = gi * EXPERTS_PER_GROUP + l1
    e2 = gi * EXPERTS_PER_GROUP + l2
    mw_ref[0:1, rs] = s1 / (s1 + s2)
    mw_ref[1:2, rs] = s2 / (s1 + s2)

    eio = lax.broadcasted_iota(jnp.int32, (n_exp, tm), 0).astype(F32)
    oh1 = eio == e1
    oh2 = eio == e2
    onehot = jnp.where(oh1 | oh2, 1.0, 0.0)
    tr = lax.broadcasted_iota(jnp.int32, (tm, tm), 0)
    tc = lax.broadcasted_iota(jnp.int32, (tm, tm), 1)
    before = jnp.where(tr < tc, 1.0, 0.0).astype(BF16)
    pref = _dot(onehot.astype(BF16), before) + base
    r1 = jnp.sum(jnp.where(oh1, pref, 0.0), axis=0, keepdims=True)
    r2 = jnp.sum(jnp.where(oh2, pref, 0.0), axis=0, keepdims=True)
    mi_ref[0:1, rs] = e1.astype(jnp.int32)
    mi_ref[1:2, rs] = e2.astype(jnp.int32)
    mi_ref[2:3, rs] = r1.astype(jnp.int32)
    mi_ref[3:4, rs] = r2.astype(jnp.int32)
    return base + jnp.sum(onehot, axis=1, keepdims=True)


def _out_projection(x2, ya, yb, proj, w_out_b, g1, nw2, sc2, sh2, rwt, rb, seq):
    n, d = x2.shape
    n_exp = rwt.shape[0]
    tm = min(OUT_TM, seq)
    tpb = seq // tm
    bvec = pl.BlockSpec((1, 1, d), lambda i: (i // tpb, 0, 0))
    return pl.pallas_call(
        _outproj_kernel,
        grid=(n // tm,),
        in_specs=[
            pl.BlockSpec((tm, d), lambda i: (i, 0)),
            pl.BlockSpec((tm, d), lambda i: (i, 0)),
            pl.BlockSpec((tm, d), lambda i: (i, 0)),
            pl.BlockSpec((tm, d), lambda i: (i, 6)),
            pl.BlockSpec((tm, d), lambda i: (i, 7)),
            pl.BlockSpec((d, d), lambda i: (0, 0)),
            bvec,
            pl.BlockSpec((1, d), lambda i: (0, 0)),
            bvec, bvec,
            pl.BlockSpec((n_exp, d), lambda i: (0, 0)),
            pl.BlockSpec((n_exp, 1), lambda i: (0, 0)),
        ],
        out_specs=[
            pl.BlockSpec((tm, d), lambda i: (i, 0)),
            pl.BlockSpec((tm, d), lambda i: (i, 0)),
            pl.BlockSpec((4, tm), lambda i: (0, i)),
            pl.BlockSpec((2, tm), lambda i: (0, i)),
            pl.BlockSpec((n_exp, LANES), lambda i: (0, 0)),
        ],
        out_shape=[
            jax.ShapeDtypeStruct((n, d), F32),
            jax.ShapeDtypeStruct((n, d), F32),
            jax.ShapeDtypeStruct((4, n), jnp.int32),
            jax.ShapeDtypeStruct((2, n), F32),
            jax.ShapeDtypeStruct((n_exp, LANES), F32),
        ],
        scratch_shapes=[pltpu.VMEM((n_exp, 1), F32)],
        compiler_params=_cparams(("arbitrary",)),
        name="merge_outproj_router",
    )(x2, ya, yb, proj, proj, w_out_b, g1, nw2, sc2, sh2, rwt, rb)


def _row_copy(src_ref, src_row, dst_ref, dst_row, sem):
    return pltpu.make_async_copy(src_ref.at[pl.ds(src_row, 1)], dst_ref.at[pl.ds(dst_row, 1)], sem)


def _dispatch_kernel(pend_ref, dest_ref, h_ref, xb_ref, zero_scr, sem, zero_sem):
    tm = h_ref.shape[0]
    n_exp = pend_ref.shape[0]

    @pl.when(pl.program_id(0) == 0)
    def _():
        zero_scr[...] = jnp.zeros(zero_scr.shape, F32)

        def clear(row0):
            row0 = pl.multiple_of(row0, MOE_ROWS)
            return pltpu.make_async_copy(zero_scr, xb_ref.at[pl.ds(row0, MOE_ROWS)], zero_sem)

        def last_block(e):
            return clear(jnp.maximum(pend_ref[e] - MOE_ROWS, 0))

        for e in range(n_exp):
            last_block(e).start()
        for e in range(n_exp):
            last_block(e).wait()

        n_used = pend_ref[n_exp - 1] // MOE_ROWS
        n_blocks = xb_ref.shape[0] // MOE_ROWS

        def start_tail(j, carry):
            clear(j * MOE_ROWS).start()
            return carry

        def wait_tail(j, carry):
            clear(j * MOE_ROWS).wait()
            return carry

        lax.fori_loop(n_used, n_blocks, start_tail, 0)
        lax.fori_loop(n_used, n_blocks, wait_tail, 0)

    def start(t, carry):
        _row_copy(h_ref, t, xb_ref, dest_ref[0, t], sem).start()
        _row_copy(h_ref, t, xb_ref, dest_ref[1, t], sem).start()
        return carry

    lax.fori_loop(0, tm, start, 0, unroll=DMA_UNROLL)
    for _ in range(2):
        pltpu.make_async_copy(h_ref, xb_ref.at[pl.ds(0, tm)], sem).wait()


def _dispatch(pend, dest, h2, n_rows):
    n, d = h2.shape
    tm = min(TOKEN_TILE, n)
    grid_spec = pltpu.PrefetchScalarGridSpec(
        num_scalar_prefetch=1,
        grid=(n // tm,),
        in_specs=[
            pl.BlockSpec((2, tm), lambda i, pe: (0, i), memory_space=pltpu.SMEM),
            pl.BlockSpec((tm, d), lambda i, pe: (i, 0)),
        ],
        out_specs=pl.BlockSpec(memory_space=pl.ANY),
        scratch_shapes=[
            pltpu.VMEM((MOE_ROWS, d), F32),
            pltpu.SemaphoreType.DMA(()),
            pltpu.SemaphoreType.DMA(()),
        ],
    )
    return pl.pallas_call(
        _dispatch_kernel,
        grid_spec=grid_spec,
        out_shape=jax.ShapeDtypeStruct((n_rows, d), F32),
        compiler_params=_cparams(("arbitrary",)),
        name="moe_dispatch",
    )(pend, dest, h2)


def _expert_kernel(be_ref, nu_ref, x_ref, wg_ref, wu_ref, wd_ref, o_ref):
    del be_ref
    j = pl.program_id(0)
    k = pl.program_id(1)
    used = j < nu_ref[0]

    def partial_out():
        x = x_ref[...].astype(BF16)
        wg = wg_ref[0, 0].astype(BF16)
        wu = wu_ref[0, 0].astype(BF16)
        hid = _silu(_dot(x, wg)) * _dot(x, wu)
        return _dot(hid.astype(BF16), wd_ref[0, 0].astype(BF16))

    @pl.when(used & (k == 0))
    def _():
        o_ref[...] = partial_out()

    @pl.when(used & (k != 0))
    def _():
        o_ref[...] += partial_out()

    @pl.when(jnp.logical_not(used) & (k == 0))
    def _():
        o_ref[...] = jnp.zeros(o_ref.shape, F32)


def _experts(layer, block_e, n_used, xb, w_gate, w_up, w_down):
    n_rows, d = xb.shape
    f = w_gate.shape[3]
    fh = f // 2
    nb = n_rows // MOE_ROWS

    def blk(j, nu):
        return jnp.minimum(j, nu[0] - 1)

    def half(j, k, nu):
        kk = jnp.where(j < nu[0], k, 1)
        return jnp.where(blk(j, nu) % 2 == 0, kk, 1 - kk)

    grid_spec = pltpu.PrefetchScalarGridSpec(
        num_scalar_prefetch=2,
        grid=(nb, 2),
        in_specs=[
            pl.BlockSpec((MOE_ROWS, d), lambda j, k, be, nu: (blk(j, nu), 0)),
            pl.BlockSpec((1, 1, d, fh), lambda j, k, be, nu: (layer, be[blk(j, nu)], 0, half(j, k, nu))),
            pl.BlockSpec((1, 1, d, fh), lambda j, k, be, nu: (layer, be[blk(j, nu)], 0, half(j, k, nu))),
            pl.BlockSpec((1, 1, fh, d), lambda j, k, be, nu: (layer, be[blk(j, nu)], half(j, k, nu), 0)),
        ],
        out_specs=pl.BlockSpec((MOE_ROWS, d), lambda j, k, be, nu: (j, 0)),
    )
    return pl.pallas_call(
        _expert_kernel,
        grid_spec=grid_spec,
        out_shape=jax.ShapeDtypeStruct((n_rows, d), F32),
        compiler_params=_cparams(("arbitrary", "arbitrary")),
        name="moe_experts",
    )(block_e, n_used, xb, w_gate, w_up, w_down)


def _combine_kernel(final, dest_ref, x_ref, wt_ref, g2_ref, fw_ref, yb_ref, o_ref, r0_scr, r1_scr, sem):
    tm = x_ref.shape[0]

    def start(t, carry):
        _row_copy(yb_ref, dest_ref[0, t], r0_scr, t, sem).start()
        _row_copy(yb_ref, dest_ref[1, t], r1_scr, t, sem).start()
        return carry

    lax.fori_loop(0, tm, start, 0, unroll=DMA_UNROLL)
    pltpu.make_async_copy(yb_ref.at[pl.ds(0, tm)], r0_scr, sem).wait()
    pltpu.make_async_copy(yb_ref.at[pl.ds(0, tm)], r1_scr, sem).wait()

    wt = wt_ref[...]
    ffn = wt[:, 0:1] * r0_scr[...] + wt[:, 1:2] * r1_scr[...]
    xn = x_ref[...] + g2_ref[0] * ffn
    if final:
        ms = jnp.mean(xn * xn, axis=-1, keepdims=True)
        xn = xn * lax.rsqrt(ms + EPS) * fw_ref[...]
    o_ref[...] = xn


def _combine(dest, x2, wt, g2, fw, yb, seq, final):
    n, d = x2.shape
    tm = min(TOKEN_TILE, seq)
    tpb = seq // tm
    return pl.pallas_call(
        functools.partial(_combine_kernel, final),
        grid=(n // tm,),
        in_specs=[
            pl.BlockSpec((2, tm), lambda i: (0, i), memory_space=pltpu.SMEM),
            pl.BlockSpec((tm, d), lambda i: (i, 0)),
            pl.BlockSpec((tm, 2), lambda i: (i, 0)),
            pl.BlockSpec((1, 1, d), lambda i: (i // tpb, 0, 0)),
            pl.BlockSpec((1, d), lambda i: (0, 0)),
            pl.BlockSpec(memory_space=pl.ANY),
        ],
        out_specs=pl.BlockSpec((tm, d), lambda i: (i, 0)),
        out_shape=jax.ShapeDtypeStruct((n, d), F32),
        scratch_shapes=[
            pltpu.VMEM((tm, d), F32),
            pltpu.VMEM((tm, d), F32),
            pltpu.SemaphoreType.DMA(()),
        ],
        compiler_params=_cparams(("arbitrary",)),
        name="moe_combine",
    )(dest, x2, wt, g2, fw, yb)


def _layer_params(l, d, w_in, conv_w, conv_b, dt_bias, a_log, d_skip, ssm_norm_w,
                  lb_all, hgrn_norm_w):
    gn = SSM_GROUPS * SSM_STATE
    n_heads = d // SSM_HEAD_DIM
    o_b, o_c, o_dt = 2 * d, 2 * d + gn, 2 * d + 2 * gn
    o_rest = o_dt + n_heads
    w = w_in[l]
    w_r = _relayout_w_in(w, d)
    w_dt = jnp.pad(w[:, o_dt:o_rest], ((0, 0), (0, LANES - n_heads))).astype(BF16)
    pad_h = lambda v: jnp.pad(v.astype(F32), (0, LANES - n_heads)).reshape(1, LANES)
    lb = jnp.maximum(lb_all[l], 0.0)
    head_of_col = jnp.arange(d, dtype=jnp.int32) // SSM_HEAD_DIM
    head_expand = (jnp.arange(LANES, dtype=jnp.int32)[:, None] == head_of_col[None, :]).astype(BF16)
    cw, cb = conv_w[l], conv_b[l]
    return dict(
        w_r=w_r, w_dt=w_dt,
        cwx=cw[:, :d], cwb=cw[:, d:d + gn], cwc=cw[:, d + gn:],
        cbx=cb[:d].reshape(1, d), cbb=cb[d:d + gn].reshape(1, gn), cbc=cb[d + gn:].reshape(1, gn),
        dt_bias=pad_h(dt_bias[l]), a_log=pad_h(a_log[l]),
        d_skip=jnp.repeat(d_skip[l].astype(F32), SSM_HEAD_DIM).reshape(1, d),
        ssm_norm_w=ssm_norm_w[l].reshape(1, d),
        head_expand=head_expand,
        log_lb=(jnp.log(lb) * LOG2E).reshape(1, d), log_1m_lb=(jnp.log1p(-lb) * LOG2E).reshape(1, d),
        hgrn_norm_w=hgrn_norm_w[l].reshape(1, d),
    )


def kernel(x, c, ada_w, ada_b, norm1_w, norm2_w, w_in, conv_w, conv_b, dt_bias, a_log, d_skip,
           ssm_norm_w, hgrn_lower_bounds, hgrn_norm_w, w_out, router_w, router_bias,
           w_gate, w_up, w_down, final_norm_w):
    bsz, seq, d = x.shape
    n_layers = ada_w.shape[0]
    n_exp = router_w.shape[1]
    n = bsz * seq
    n_assign = 2 * n
    nb = n_assign // MOE_ROWS + n_exp
    n_rows = nb * MOE_ROWS

    lb_all = jnp.cumsum(jax.nn.softmax(hgrn_lower_bounds.astype(F32), axis=0), axis=0)
    lb_all = lb_all - lb_all[0:1]
    mod = _modulation(c, ada_w, ada_b)
    rwt = router_w.astype(F32).T
    rb = router_bias.astype(F32).reshape(n_exp, 1)
    fw = final_norm_w.reshape(1, d)

    x2 = x.reshape(n, d)
    for l in range(n_layers):
        sh1, sc1, g1, sh2, sc2, g2 = [mod[l, :, k * d:(k + 1) * d].reshape(bsz, 1, d) for k in range(6)]
        p = _layer_params(l, d, w_in, conv_w, conv_b, dt_bias, a_log, d_skip, ssm_norm_w,
                          lb_all, hgrn_norm_w)
        proj, dt_raw = _in_projection(x2, norm1_w[l].reshape(1, d), sc1, sh1, p["w_r"], p["w_dt"], seq)
        ya = _ssd_branch(proj, dt_raw, p, bsz, seq, d)
        yb = _hgrn_branch(proj, p, bsz, seq, d)
        x2, h2, mi, mw, cnt = _out_projection(
            x2, ya, yb, proj, w_out[l].astype(BF16), g1, norm2_w[l].reshape(1, d), sc2, sh2, rwt, rb, seq)

        counts = cnt[:, 0].astype(jnp.int32)
        padded = (counts + MOE_ROWS - 1) // MOE_ROWS * MOE_ROWS
        pend = jnp.cumsum(padded)
        pstart = pend - padded
        is_e = mi[0:2, :, None] == jnp.arange(n_exp, dtype=jnp.int32)
        dest = jnp.sum(jnp.where(is_e, pstart, 0), axis=-1) + mi[2:4]
        block_row0 = jnp.arange(nb, dtype=jnp.int32) * MOE_ROWS
        block_e = jnp.minimum(
            jnp.sum((pend[None, :] <= block_row0[:, None]).astype(jnp.int32), axis=1), n_exp - 1)
        n_used = (pend[-1:] // MOE_ROWS).astype(jnp.int32)

        xb = _dispatch(pend.astype(jnp.int32), dest, h2, n_rows)
        yexp = _experts(l, block_e, n_used, xb, w_gate, w_up, w_down)
        x2 = _combine(dest, x2, mw.T, g2, fw, yexp, seq, final=(l == n_layers - 1))
    return x2.reshape(bsz, seq, d)
```

```python
import functools

import jax
import jax.numpy as jnp
from jax import lax
from jax.experimental import pallas as pl
from jax.experimental.pallas import tpu as pltpu

F32 = jnp.float32
BF16 = jnp.bfloat16
EPS = 1e-6
LOG2E = 1.4426950408889634

SSM_HEAD_DIM = 64
SSM_STATE = 128
SSM_GROUPS = 4
CONV_K = 4
HGRN_HEAD = 128
N_EXPERT_GROUPS = 4
EXPERTS_PER_GROUP = 8
LANES = 128
SUBLANES = 8

SSD_CHUNK = 128
HGRN_CHUNK = 128
HGRN_HEADS_PER_STEP = 8
HGRN_CHUNKS_PER_STEP = 2
MOE_ROWS = 512
INPROJ_TM = 1024
INPROJ_TN = 1024
OUT_TM = 256
ROUTE_ROWS = 128
TOKEN_TILE = 256
DMA_UNROLL = 8
VMEM_LIMIT = 56 * 1024 * 1024


def _cparams(sem):
    return pltpu.CompilerParams(dimension_semantics=sem, vmem_limit_bytes=VMEM_LIMIT)


def _sigmoid(x):
    return 1.0 / (1.0 + jnp.exp(-x))


def _silu(x):
    return x * _sigmoid(x)


def _split3(x):
    hi = x.astype(BF16)
    r1 = x - hi.astype(F32)
    mid = r1.astype(BF16)
    lo = (r1 - mid.astype(F32)).astype(BF16)
    return hi, mid, lo


def _dot(a, b):
    return jnp.dot(a, b, preferred_element_type=F32)


def _dot_nt(a, b):
    return lax.dot_general(a, b, (((1,), (1,)), ((), ())), preferred_element_type=F32)


def _sel_dot(sel_bf16, x):
    hi, mid, lo = _split3(x)
    return _dot(sel_bf16, hi) + (_dot(sel_bf16, mid) + _dot(sel_bf16, lo))


def _dot_sel(x, sel_bf16):
    hi, mid, lo = _split3(x)
    return _dot(hi, sel_bf16) + (_dot(mid, sel_bf16) + _dot(lo, sel_bf16))


def _tril_ones(n):
    r = lax.broadcasted_iota(jnp.int32, (n, n), 0)
    c = lax.broadcasted_iota(jnp.int32, (n, n), 1)
    return jnp.where(r >= c, 1.0, 0.0).astype(BF16)


def _mod_kernel(c_ref, w_ref, b_ref, o_ref):
    ca = _silu(c_ref[...]).astype(BF16)
    o_ref[0] = _dot(ca, w_ref[0].astype(BF16)) + b_ref[0]


def _modulation(c, ada_w, ada_b):
    n_layers, d, n6 = ada_w.shape
    bsz = c.shape[0]
    tn = 1024
    return pl.pallas_call(
        _mod_kernel,
        grid=(n_layers, n6 // tn),
        in_specs=[
            pl.BlockSpec((bsz, d), lambda l, j: (0, 0)),
            pl.BlockSpec((1, d, tn), lambda l, j: (l, 0, j)),
            pl.BlockSpec((1, 1, tn), lambda l, j: (l, 0, j)),
        ],
        out_specs=pl.BlockSpec((1, bsz, tn), lambda l, j: (l, 0, j)),
        out_shape=jax.ShapeDtypeStruct((n_layers, bsz, n6), F32),
        compiler_params=_cparams(("arbitrary", "arbitrary")),
        name="adaln_mod",
    )(c, ada_w, ada_b.reshape(n_layers, 1, n6))


def _relayout_kernel(shift, src_ref, shifted_ref, wa_ref, wb_ref, wdt_ref, o_ref, odt_ref):
    del src_ref
    t = pl.program_id(0)

    @pl.when(t == 0)
    def _():
        lane = lax.broadcasted_iota(jnp.int32, wdt_ref.shape[1:], 1)
        odt_ref[...] = jnp.where(lane < shift, wdt_ref[0], 0.0).astype(BF16)

    @pl.when(shifted_ref[t] == 0)
    def _():
        o_ref[...] = wa_ref[0].astype(BF16)

    @pl.when(shifted_ref[t] != 0)
    def _():
        o_ref[...] = jnp.concatenate([wa_ref[0, :, shift:], wb_ref[0, :, :shift]], axis=1).astype(BF16)


def _relayout_w_in(w_in, layer):
    d = w_in.shape[1]
    gn2 = 2 * SSM_GROUPS * SSM_STATE
    n_heads = d // SSM_HEAD_DIM
    tn = INPROJ_TN
    assert gn2 == tn and (2 * d) % tn == 0 and 0 < n_heads < LANES
    n_front, n_rest = 2 * d // tn, 6 * d // tn
    src = list(range(n_front)) + [n_front + 1 + k for k in range(n_rest)] + [n_front]
    shifted = [0] * n_front + [1] * n_rest + [0]
    n_tiles = len(src)
    per = tn // LANES
    grid_spec = pltpu.PrefetchScalarGridSpec(
        num_scalar_prefetch=2,
        grid=(n_tiles,),
        in_specs=[
            pl.BlockSpec((1, d, tn), lambda t, s, f: (layer, 0, s[t])),
            pl.BlockSpec((1, d, LANES), lambda t, s, f: (layer, 0, (s[t] + 1) * per)),
            pl.BlockSpec((1, d, LANES), lambda t, s, f: (layer, 0, (n_front + 1) * per)),
        ],
        out_specs=[
            pl.BlockSpec((d, tn), lambda t, s, f: (0, t)),
            pl.BlockSpec((d, LANES), lambda t, s, f: (0, 0)),
        ],
    )
    return pl.pallas_call(
        functools.partial(_relayout_kernel, n_heads),
        grid_spec=grid_spec,
        out_shape=[
            jax.ShapeDtypeStruct((d, n_tiles * tn), BF16),
            jax.ShapeDtypeStruct((d, LANES), BF16),
        ],
        compiler_params=_cparams(("arbitrary",)),
        name="w_in_relayout",
    )(jnp.asarray(src, jnp.int32), jnp.asarray(shifted, jnp.int32), w_in, w_in, w_in)


def _rms_mod(x, nw, sc, sh):
    ms = jnp.mean(x * x, axis=-1, keepdims=True)
    return (x * lax.rsqrt(ms + EPS) * nw) * (1.0 + sc) + sh


def _inproj_kernel(x_ref, nw_ref, sc_ref, sh_ref, w_ref, wdt_ref, o_ref, dt_ref, h_scr):
    @pl.when(pl.program_id(1) == 0)
    def _():
        h = _rms_mod(x_ref[...], nw_ref[...], sc_ref[0], sh_ref[0]).astype(BF16)
        h_scr[...] = h
        dt_ref[...] = _dot(h, wdt_ref[...])

    o_ref[...] = _dot(h_scr[...], w_ref[...])


def _in_projection(x2, nw, sc, sh, w_r, w_dt, seq):
    n, d = x2.shape
    np_ = w_r.shape[1]
    tm = min(INPROJ_TM, seq)
    tn = INPROJ_TN
    tiles_per_batch = seq // tm
    return pl.pallas_call(
        _inproj_kernel,
        grid=(n // tm, np_ // tn),
        in_specs=[
            pl.BlockSpec((tm, d), lambda i, j: (i, 0)),
            pl.BlockSpec((1, d), lambda i, j: (0, 0)),
            pl.BlockSpec((1, 1, d), lambda i, j: (i // tiles_per_batch, 0, 0)),
            pl.BlockSpec((1, 1, d), lambda i, j: (i // tiles_per_batch, 0, 0)),
            pl.BlockSpec((d, tn), lambda i, j: (0, j)),
            pl.BlockSpec((d, LANES), lambda i, j: (0, 0)),
        ],
        out_specs=[
            pl.BlockSpec((tm, tn), lambda i, j: (i, j)),
            pl.BlockSpec((tm, LANES), lambda i, j: (i, 0)),
        ],
        out_shape=[
            jax.ShapeDtypeStruct((n, np_), F32),
            jax.ShapeDtypeStruct((n, LANES), F32),
        ],
        scratch_shapes=[pltpu.VMEM((tm, d), BF16)],
        compiler_params=_cparams(("arbitrary", "arbitrary")),
        name="norm_inproj",
    )(x2, nw, sc, sh, w_r, w_dt)


def _causal_conv(ext_ref, u, w_ref, b_ref, first):
    q = u.shape[0]

    @pl.when(first)
    def _():
        ext_ref[0:SUBLANES, :] = jnp.zeros((SUBLANES, u.shape[1]), F32)

    ext_ref[SUBLANES:SUBLANES + q, :] = u
    acc = b_ref[...] + w_ref[CONV_K - 1:CONV_K, :] * u
    for k in range(CONV_K - 1):
        off = SUBLANES - (CONV_K - 1) + k
        acc = acc + w_ref[k:k + 1, :] * ext_ref[off:off + q, :]
    ext_ref[0:SUBLANES, :] = ext_ref[q:q + SUBLANES, :]
    return _silu(acc)


def _ssd_kernel(z_ref, xs_ref, bm_ref, cm_ref, dt_ref,
                cwx_ref, cwb_ref, cwc_ref, cbx_ref, cbb_ref, cbc_ref,
                dtb_ref, alog_ref, dskip_ref, nw_ref, exp_ref,
                o_ref,
                state_scr, extx_scr, extb_scr, extc_scr):
    q = z_ref.shape[0]
    d_inner = z_ref.shape[1]
    gw = d_inner // SSM_GROUPS
    heads_per_group = gw // SSM_HEAD_DIM
    first = pl.program_id(1) == 0

    @pl.when(first)
    def _():
        state_scr[...] = jnp.zeros(state_scr.shape, F32)

    xs = _causal_conv(extx_scr, xs_ref[...], cwx_ref, cbx_ref, first)
    bmat = _causal_conv(extb_scr, bm_ref[...], cwb_ref, cbb_ref, first)
    cmat = _causal_conv(extc_scr, cm_ref[...], cwc_ref, cbc_ref, first)

    x_dt = dt_ref[...] + dtb_ref[...]
    dt = jnp.maximum(x_dt, 0.0) + jnp.log(1.0 + jnp.exp(-jnp.abs(x_dt)))
    a = -jnp.exp(alog_ref[...])
    acs = _sel_dot(_tril_ones(q), dt * a)
    acs_last = acs[q - 1:q, :]
    stack = jnp.concatenate(
        [dt, jnp.exp(acs), jnp.exp(acs_last - acs),
         jnp.broadcast_to(jnp.exp(acs_last), (SUBLANES, LANES))], axis=0)
    wide = _dot_sel(stack, exp_ref[...])
    dt_e = wide[0:q]
    eacs_e = wide[q:2 * q]
    dec_e = wide[2 * q:3 * q]
    cd_e = wide[3 * q:3 * q + 1]

    xdt = xs * dt_e
    xdt_b = xdt.astype(BF16)
    xdec_b = (xdt * dec_e).astype(BF16)

    acs_t = acs.T
    ri = lax.broadcasted_iota(jnp.int32, (q, q), 0)
    ci = lax.broadcasted_iota(jnp.int32, (q, q), 1)
    causal = ri >= ci
    lane = lax.broadcasted_iota(jnp.int32, (q, LANES), 1)
    lo_half = lane < SSM_HEAD_DIM

    y_parts = []
    for g in range(SSM_GROUPS):
        bg = bmat[:, g * SSM_STATE:(g + 1) * SSM_STATE]
        cg = cmat[:, g * SSM_STATE:(g + 1) * SSM_STATE]
        bg_b = bg.astype(BF16)
        cg_b = cg.astype(BF16)
        cb = _dot_nt(cg_b, bg_b)
        st = state_scr[g]
        y_off = _dot(cg_b, st.astype(BF16)) * eacs_e[:, g * gw:(g + 1) * gw]
        diag = []
        for pair in range(heads_per_group // 2):
            ms = []
            for sub in range(2):
                h = g * heads_per_group + pair * 2 + sub
                rel = acs[:, h:h + 1] - acs_t[h:h + 1, :]
                lmat = jnp.exp(jnp.where(causal, rel, -jnp.inf))
                ms.append((cb * lmat).astype(BF16))
            lhs = jnp.concatenate(ms, axis=1)
            c0 = (g * heads_per_group + pair * 2) * SSM_HEAD_DIM
            xp = xdt_b[:, c0:c0 + LANES]
            zero = jnp.zeros_like(xp)
            rhs = jnp.concatenate(
                [jnp.where(lo_half, xp, zero), jnp.where(lo_half, zero, xp)], axis=0)
            diag.append(_dot(lhs, rhs))
        y_parts.append(jnp.concatenate(diag, axis=1) + y_off)
        new_state = st * cd_e[:, g * gw:(g + 1) * gw] + _dot(
            bg.T.astype(BF16), xdec_b[:, g * gw:(g + 1) * gw])
        state_scr[g] = new_state

    y = jnp.concatenate(y_parts, axis=1) + dskip_ref[...] * xs
    y = y * _silu(z_ref[...])
    outs = []
    for g in range(SSM_GROUPS):
        yg = y[:, g * gw:(g + 1) * gw]
        ms = jnp.mean(yg * yg, axis=-1, keepdims=True)
        outs.append(yg * lax.rsqrt(ms + EPS))
    o_ref[...] = (jnp.concatenate(outs, axis=1) * nw_ref[...]).astype(o_ref.dtype)


def _ssd_branch(proj, dt_raw, p, bsz, seq, d):
    q = min(SSD_CHUNK, seq)
    nc = seq // q
    gn = SSM_GROUPS * SSM_STATE
    b_blk = 8 * d // gn
    row = lambda b, c: b * nc + c
    vec = lambda width: pl.BlockSpec((1, width), lambda b, c: (0, 0))
    return pl.pallas_call(
        _ssd_kernel,
        grid=(bsz, nc),
        in_specs=[
            pl.BlockSpec((q, d), lambda b, c: (row(b, c), 0)),
            pl.BlockSpec((q, d), lambda b, c: (row(b, c), 1)),
            pl.BlockSpec((q, gn), lambda b, c: (row(b, c), b_blk)),
            pl.BlockSpec((q, gn), lambda b, c: (row(b, c), b_blk + 1)),
            pl.BlockSpec((q, LANES), lambda b, c: (row(b, c), 0)),
            pl.BlockSpec((CONV_K, d), lambda b, c: (0, 0)),
            pl.BlockSpec((CONV_K, gn), lambda b, c: (0, 0)),
            pl.BlockSpec((CONV_K, gn), lambda b, c: (0, 0)),
            vec(d), vec(gn), vec(gn),
            vec(LANES), vec(LANES), vec(d), vec(d),
            pl.BlockSpec((LANES, d), lambda b, c: (0, 0)),
        ],
        out_specs=pl.BlockSpec((q, d), lambda b, c: (row(b, c), 0)),
        out_shape=jax.ShapeDtypeStruct((bsz * seq, d), BF16),
        scratch_shapes=[
            pltpu.VMEM((SSM_GROUPS, SSM_STATE, d // SSM_GROUPS), F32),
            pltpu.VMEM((q + SUBLANES, d), F32),
            pltpu.VMEM((q + SUBLANES, gn), F32),
            pltpu.VMEM((q + SUBLANES, gn), F32),
        ],
        compiler_params=_cparams(("arbitrary", "arbitrary")),
        name="ssd_scan",
    )(proj, proj, proj, proj, dt_raw,
      p["cwx"], p["cwb"], p["cwc"], p["cbx"], p["cbb"], p["cbc"],
      p["dt_bias"], p["a_log"], p["d_skip"], p["ssm_norm_w"], p["head_expand"])


def _hgrn_chunk(q_raw, f_raw, v, g_raw, loglb, log1mlb, nw, s_t, tril, xor_idx, below):
    c = q_raw.shape[0]
    nv = c // SUBLANES
    f2 = f_raw * LOG2E
    logsig = jnp.minimum(f2, 0.0) - jnp.log2(1.0 + jnp.exp2(-jnp.abs(f2)))
    bb = log1mlb + logsig
    logf2 = jnp.maximum(loglb, bb) + jnp.log2(1.0 + jnp.exp2(-jnp.abs(loglb - bb)))
    kk = jnp.exp2(bb - f2)
    qf = _silu(q_raw)
    b = _sel_dot(tril, logf2)

    b3 = b.reshape(nv, SUBLANES, LANES)
    q3 = qf.reshape(nv, SUBLANES, LANES)
    k3 = kk.reshape(nv, SUBLANES, LANES)
    sub = lax.broadcasted_iota(jnp.int32, (1, SUBLANES, LANES), 1)

    def brow(i):
        return jnp.broadcast_to(b3[:, i:i + 1, :], b3.shape)

    levels = []
    up = (sub & 1) != 0
    levels.append((1, jnp.where(up, q3 * jnp.exp2(logf2.reshape(b3.shape)), k3).reshape(c, LANES)))
    up = (sub & 2) != 0
    d = (b3 - jnp.where(sub < 4, brow(1), brow(5))) * jnp.where(up, 1.0, -1.0)
    levels.append((2, (jnp.where(up, q3, k3) * jnp.exp2(d)).reshape(c, LANES)))
    up = sub >= 4
    d = (b3 - brow(3)) * jnp.where(up, 1.0, -1.0)
    levels.append((4, (jnp.where(up, q3, k3) * jnp.exp2(d)).reshape(c, LANES)))
    w = SUBLANES
    while w < c:
        parts = []
        for j in range(c // (2 * w)):
            lo, mid, hi = 2 * w * j, 2 * w * j + w, 2 * w * (j + 1)
            bm = b[mid - 1:mid, :]
            parts.append(kk[lo:mid] * jnp.exp2(bm - b[lo:mid]))
            parts.append(qf[mid:hi] * jnp.exp2(b[mid:hi] - bm))
        levels.append((w, jnp.concatenate(parts, axis=0)))
        w *= 2

    scores = None
    for w, xw in reversed(levels):
        xb = xw.astype(BF16)
        a_w = _dot_nt(xb, xb)
        scores = a_w if scores is None else jnp.where(xor_idx < 2 * w, a_w, scores)
    scores = jnp.where(below, scores, 0.0)

    o = _dot(scores.astype(BF16), v.astype(BF16))
    o = o + jnp.sum(qf * kk, axis=-1, keepdims=True) * v
    o = o + _dot_nt((qf * jnp.exp2(b)).astype(BF16), s_t.astype(BF16))
    b_last = b[c - 1:c, :]
    kst = (kk * jnp.exp2(b_last - b)).astype(BF16)
    s_new = s_t * jnp.exp2(b_last) + _dot(v.T.astype(BF16), kst)

    ms = jnp.mean(o * o, axis=-1, keepdims=True)
    y = o * lax.rsqrt(ms + EPS) * nw * _silu(g_raw)
    return y, s_new


def _hgrn_kernel(q_ref, f_ref, i_ref, g_ref, loglb_ref, log1mlb_ref, nw_ref, o_ref, state_scr):
    rows = q_ref.shape[0]
    heads = q_ref.shape[1] // HGRN_HEAD
    c = min(HGRN_CHUNK, rows)

    @pl.when(pl.program_id(2) == 0)
    def _():
        state_scr[...] = jnp.zeros(state_scr.shape, F32)

    tril = _tril_ones(c)
    ri = lax.broadcasted_iota(jnp.int32, (c, c), 0)
    ci = lax.broadcasted_iota(jnp.int32, (c, c), 1)
    xor_idx = ri ^ ci
    below = ri > ci
    for h in range(heads):
        cols = slice(h * HGRN_HEAD, (h + 1) * HGRN_HEAD)
        s_t = state_scr[h]
        for k in range(rows // c):
            rs = slice(k * c, (k + 1) * c)
            y, s_t = _hgrn_chunk(q_ref[rs, cols], f_ref[rs, cols], i_ref[rs, cols], g_ref[rs, cols],
                                 loglb_ref[:, cols], log1mlb_ref[:, cols], nw_ref[:, cols], s_t,
                                 tril, xor_idx, below)
            o_ref[rs, cols] = y.astype(o_ref.dtype)
        state_scr[h] = s_t


def _hgrn_branch(proj, p, bsz, seq, d):
    n_heads = d // HGRN_HEAD
    hb = min(HGRN_HEADS_PER_STEP, n_heads)
    wblk = hb * HGRN_HEAD
    rows = min(HGRN_CHUNK * HGRN_CHUNKS_PER_STEP, seq)
    nt = seq // rows
    per = d // wblk

    def sect(k):
        return pl.BlockSpec((rows, wblk), lambda b, hg, t: (b * nt + t, k * per + hg))

    vec = pl.BlockSpec((1, wblk), lambda b, hg, t: (0, hg))
    return pl.pallas_call(
        _hgrn_kernel,
        grid=(bsz, n_heads // hb, nt),
        in_specs=[sect(2), sect(3), sect(4), sect(5), vec, vec, vec],
        out_specs=pl.BlockSpec((rows, wblk), lambda b, hg, t: (b * nt + t, hg)),
        out_shape=jax.ShapeDtypeStruct((bsz * seq, d), BF16),
        scratch_shapes=[pltpu.VMEM((hb, HGRN_HEAD, HGRN_HEAD), F32)],
        compiler_params=_cparams(("arbitrary", "arbitrary", "arbitrary")),
        name="hgrn2_scan",
    )(proj, proj, proj, proj, p["log_lb"], p["log_1m_lb"], p["hgrn_norm_w"])


def _first_argmax(vals, idx, n):
    m = jnp.max(vals, axis=0, keepdims=True)
    i = jnp.min(jnp.where(vals == m, idx, float(n)), axis=0, keepdims=True)
    return m, i


def _outproj_kernel(x_ref, ya_ref, yb_ref, ga_ref, gb_ref, wout_ref, g1_ref,
                    nw2_ref, sc2_ref, sh2_ref, rwt_ref, rb_ref,
                    xo_ref, h2_ref, mi_ref, mw_ref, cnt_ref, cnt_scr):
    tm = x_ref.shape[0]

    @pl.when(pl.program_id(0) == 0)
    def _():
        cnt_scr[...] = jnp.zeros(cnt_scr.shape, F32)

    r_hi, r_mid, _ = _split3(rwt_ref[...])
    ts = min(ROUTE_ROWS, tm)
    total = cnt_scr[...]
    for s in range(tm // ts):
        total = _outproj_rows(slice(s * ts, (s + 1) * ts), total, (r_hi, r_mid),
                              x_ref, ya_ref, yb_ref, ga_ref, gb_ref, wout_ref, g1_ref,
                              nw2_ref, sc2_ref, sh2_ref, rb_ref, xo_ref, h2_ref, mi_ref, mw_ref)
    cnt_scr[...] = total
    cnt_ref[...] = jnp.broadcast_to(total, cnt_ref.shape)


def _outproj_rows(rs, base, r_split, x_ref, ya_ref, yb_ref, ga_ref, gb_ref, wout_ref, g1_ref,
                  nw2_ref, sc2_ref, sh2_ref, rb_ref, xo_ref, h2_ref, mi_ref, mw_ref):
    tm = rs.stop - rs.start
    n_exp = rb_ref.shape[0]
    r_hi, r_mid = r_split
    merged = (_sigmoid(ga_ref[rs, :]) * ya_ref[rs, :].astype(F32)
              + _sigmoid(gb_ref[rs, :]) * yb_ref[rs, :].astype(F32))
    mix = _dot(merged.astype(BF16), wout_ref[...])
    xn = x_ref[rs, :] + g1_ref[0] * mix
    xo_ref[rs, :] = xn
    h2 = _rms_mod(xn, nw2_ref[...], sc2_ref[0], sh2_ref[0])
    h2_ref[rs, :] = h2

    h_hi, h_mid, _ = _split3(h2)
    logits = _dot_nt(r_hi, h_hi) + (_dot_nt(r_hi, h_mid) + _dot_nt(r_mid, h_hi))
    scores = _sigmoid(logits)
    sel = scores + rb_ref[...]

    sub = lax.broadcasted_iota(jnp.int32, (EXPERTS_PER_GROUP, tm), 0).astype(F32)
    neg = -jnp.inf
    best = None
    for g in range(N_EXPERT_GROUPS):
        blk = sel[g * EXPERTS_PER_GROUP:(g + 1) * EXPERTS_PER_GROUP]
        m1, i1 = _first_argmax(blk, sub, EXPERTS_PER_GROUP)
        m2 = jnp.max(jnp.where(sub == i1, neg, blk), axis=0, keepdims=True)
        gs = m1 + m2
        if best is None:
            best, gi = gs, jnp.zeros_like(gs)
            in_sel, in_sc = blk, scores[0:EXPERTS_PER_GROUP]
        else:
            upd = gs > best
            best = jnp.where(upd, gs, best)
            gi = jnp.where(upd, float(g), gi)
            in_sel = jnp.where(upd, blk, in_sel)
            in_sc = jnp.where(upd, scores[g * EXPERTS_PER_GROUP:(g + 1) * EXPERTS_PER_GROUP], in_sc)
    _, l1 = _first_argmax(in_sel, sub, EXPERTS_PER_GROUP)
    _, l2 = _first_argmax(jnp.where(sub == l1, neg, in_sel), sub, EXPERTS_PER_GROUP)
    s1 = jnp.sum(jnp.where(sub == l1, in_sc, 0.0), axis=0, keepdims=True)
    s2 = jnp.sum(jnp.where(sub == l2, in_sc, 0.0), axis=0, keepdims=True)
    e1 = gi * EXPERTS_PER_GROUP + l1
    e2 = gi * EXPERTS_PER_GROUP + l2
    mw_ref[0:1, rs] = s1 / (s1 + s2)
    mw_ref[1:2, rs] = s2 / (s1 + s2)

    eio = lax.broadcasted_iota(jnp.int32, (n_exp, tm), 0).astype(F32)
    oh1 = eio == e1
    oh2 = eio == e2
    onehot = jnp.where(oh1 | oh2, 1.0, 0.0)
    tr = lax.broadcasted_iota(jnp.int32, (tm, tm), 0)
    tc = lax.broadcasted_iota(jnp.int32, (tm, tm), 1)
    before = jnp.where(tr < tc, 1.0, 0.0).astype(BF16)
    pref = _dot(onehot.astype(BF16), before) + base
    r1 = jnp.sum(jnp.where(oh1, pref, 0.0), axis=0, keepdims=True)
    r2 = jnp.sum(jnp.where(oh2, pref, 0.0), axis=0, keepdims=True)
    mi_ref[0:1, rs] = e1.astype(jnp.int32)
    mi_ref[1:2, rs] = e2.astype(jnp.int32)
    mi_ref[2:3, rs] = r1.astype(jnp.int32)
    mi_ref[3:4, rs] = r2.astype(jnp.int32)
    return base + jnp.sum(onehot, axis=1, keepdims=True)


def _out_projection(x2, ya, yb, proj, w_out_b, g1, nw2, sc2, sh2, rwt, rb, seq):
    n, d = x2.shape
    n_exp = rwt.shape[0]
    tm = min(OUT_TM, seq)
    tpb = seq // tm
    bvec = pl.BlockSpec((1, 1, d), lambda i: (i // tpb, 0, 0))
    return pl.pallas_call(
        _outproj_kernel,
        grid=(n // tm,),
        in_specs=[
            pl.BlockSpec((tm, d), lambda i: (i, 0)),
            pl.BlockSpec((tm, d), lambda i: (i, 0)),
            pl.BlockSpec((tm, d), lambda i: (i, 0)),
            pl.BlockSpec((tm, d), lambda i: (i, 6)),
            pl.BlockSpec((tm, d), lambda i: (i, 7)),
            pl.BlockSpec((d, d), lambda i: (0, 0)),
            bvec,
            pl.BlockSpec((1, d), lambda i: (0, 0)),
            bvec, bvec,
            pl.BlockSpec((n_exp, d), lambda i: (0, 0)),
            pl.BlockSpec((n_exp, 1), lambda i: (0, 0)),
        ],
        out_specs=[
            pl.BlockSpec((tm, d), lambda i: (i, 0)),
            pl.BlockSpec((tm, d), lambda i: (i, 0)),
            pl.BlockSpec((4, tm), lambda i: (0, i)),
            pl.BlockSpec((2, tm), lambda i: (0, i)),
            pl.BlockSpec((n_exp, LANES), lambda i: (0, 0)),
        ],
        out_shape=[
            jax.ShapeDtypeStruct((n, d), F32),
            jax.ShapeDtypeStruct((n, d), F32),
            jax.ShapeDtypeStruct((4, n), jnp.int32),
            jax.ShapeDtypeStruct((2, n), F32),
            jax.ShapeDtypeStruct((n_exp, LANES), F32),
        ],
        scratch_shapes=[pltpu.VMEM((n_exp, 1), F32)],
        compiler_params=_cparams(("arbitrary",)),
        name="merge_outproj_router",
    )(x2, ya, yb, proj, proj, w_out_b, g1, nw2, sc2, sh2, rwt, rb)


def _row_copy(src_ref, src_row, dst_ref, dst_row, sem):
    return pltpu.make_async_copy(src_ref.at[pl.ds(src_row, 1)], dst_ref.at[pl.ds(dst_row, 1)], sem)


def _dispatch_kernel(pend_ref, dest_ref, h_ref, xb_ref, zero_scr, sem, zero_sem):
    tm = h_ref.shape[0]
    n_exp = pend_ref.shape[0]

    @pl.when(pl.program_id(0) == 0)
    def _():
        zero_scr[...] = jnp.zeros(zero_scr.shape, F32)

        def clear(row0):
            row0 = pl.multiple_of(row0, MOE_ROWS)
            return pltpu.make_async_copy(zero_scr, xb_ref.at[pl.ds(row0, MOE_ROWS)], zero_sem)

        def last_block(e):
            return clear(jnp.maximum(pend_ref[e] - MOE_ROWS, 0))

        for e in range(n_exp):
            last_block(e).start()
        for e in range(n_exp):
            last_block(e).wait()

        n_used = pend_ref[n_exp - 1] // MOE_ROWS
        n_blocks = xb_ref.shape[0] // MOE_ROWS

        def start_tail(j, carry):
            clear(j * MOE_ROWS).start()
            return carry

        def wait_tail(j, carry):
            clear(j * MOE_ROWS).wait()
            return carry

        lax.fori_loop(n_used, n_blocks, start_tail, 0)
        lax.fori_loop(n_used, n_blocks, wait_tail, 0)

    def start(t, carry):
        _row_copy(h_ref, t, xb_ref, dest_ref[0, t], sem).start(priority=0)
        _row_copy(h_ref, t, xb_ref, dest_ref[1, t], sem).start(priority=1)
        return carry

    lax.fori_loop(0, tm, start, 0, unroll=DMA_UNROLL)
    for _ in range(2):
        pltpu.make_async_copy(h_ref, xb_ref.at[pl.ds(0, tm)], sem).wait()


def _dispatch(pend, dest, h2, n_rows):
    n, d = h2.shape
    tm = min(TOKEN_TILE, n)
    grid_spec = pltpu.PrefetchScalarGridSpec(
        num_scalar_prefetch=1,
        grid=(n // tm,),
        in_specs=[
            pl.BlockSpec((2, tm), lambda i, pe: (0, i), memory_space=pltpu.SMEM),
            pl.BlockSpec((tm, d), lambda i, pe: (i, 0)),
        ],
        out_specs=pl.BlockSpec(memory_space=pl.ANY),
        scratch_shapes=[
            pltpu.VMEM((MOE_ROWS, d), F32),
            pltpu.SemaphoreType.DMA(()),
            pltpu.SemaphoreType.DMA(()),
        ],
    )
    return pl.pallas_call(
        _dispatch_kernel,
        grid_spec=grid_spec,
        out_shape=jax.ShapeDtypeStruct((n_rows, d), F32),
        compiler_params=_cparams(("arbitrary",)),
        name="moe_dispatch",
    )(pend, dest, h2)


def _expert_kernel(be_ref, nu_ref, x_ref, wg_ref, wu_ref, wd_ref, o_ref):
    del be_ref
    j = pl.program_id(0)
    k = pl.program_id(1)
    used = j < nu_ref[0]

    def partial_out():
        x = x_ref[...].astype(BF16)
        wg = wg_ref[0, 0].astype(BF16)
        wu = wu_ref[0, 0].astype(BF16)
        hid = _silu(_dot(x, wg)) * _dot(x, wu)
        return _dot(hid.astype(BF16), wd_ref[0, 0].astype(BF16))

    @pl.when(used & (k == 0))
    def _():
        o_ref[...] = partial_out()

    @pl.when(used & (k != 0))
    def _():
        o_ref[...] += partial_out()

    @pl.when(jnp.logical_not(used) & (k == 0))
    def _():
        o_ref[...] = jnp.zeros(o_ref.shape, F32)


def _experts(layer, block_e, n_used, xb, w_gate, w_up, w_down):
    n_rows, d = xb.shape
    f = w_gate.shape[3]
    fh = f // 2
    nb = n_rows // MOE_ROWS

    def blk(j, nu):
        return jnp.minimum(j, nu[0] - 1)

    def half(j, k, nu):
        kk = jnp.where(j < nu[0], k, 1)
        return jnp.where(blk(j, nu) % 2 == 0, kk, 1 - kk)

    grid_spec = pltpu.PrefetchScalarGridSpec(
        num_scalar_prefetch=2,
        grid=(nb, 2),
        in_specs=[
            pl.BlockSpec((MOE_ROWS, d), lambda j, k, be, nu: (blk(j, nu), 0)),
            pl.BlockSpec((1, 1, d, fh), lambda j, k, be, nu: (layer, be[blk(j, nu)], 0, half(j, k, nu))),
            pl.BlockSpec((1, 1, d, fh), lambda j, k, be, nu: (layer, be[blk(j, nu)], 0, half(j, k, nu))),
            pl.BlockSpec((1, 1, fh, d), lambda j, k, be, nu: (layer, be[blk(j, nu)], half(j, k, nu), 0)),
        ],
        out_specs=pl.BlockSpec((MOE_ROWS, d), lambda j, k, be, nu: (j, 0)),
    )
    return pl.pallas_call(
        _expert_kernel,
        grid_spec=grid_spec,
        out_shape=jax.ShapeDtypeStruct((n_rows, d), F32),
        compiler_params=_cparams(("arbitrary", "arbitrary")),
        name="moe_experts",
    )(block_e, n_used, xb, w_gate, w_up, w_down)


def _combine_kernel(final, dest_ref, dest_next_ref, x_ref, wt_ref, g2_ref, fw_ref, yb_ref, o_ref,
                    r0_scr, r1_scr, sems):
    tm = x_ref.shape[0]
    i = pl.program_id(0)
    slot = i % 2

    def gather(idx_ref, s):
        def start(t, carry):
            _row_copy(yb_ref, idx_ref[0, t], r0_scr.at[s], t, sems.at[s]).start(priority=0)
            _row_copy(yb_ref, idx_ref[1, t], r1_scr.at[s], t, sems.at[s]).start(priority=1)
            return carry

        lax.fori_loop(0, tm, start, 0, unroll=DMA_UNROLL)

    @pl.when(i == 0)
    def _():
        gather(dest_ref, slot)

    @pl.when(i + 1 < pl.num_programs(0))
    def _():
        gather(dest_next_ref, 1 - slot)

    pltpu.make_async_copy(yb_ref.at[pl.ds(0, tm)], r0_scr.at[slot], sems.at[slot]).wait()
    pltpu.make_async_copy(yb_ref.at[pl.ds(0, tm)], r1_scr.at[slot], sems.at[slot]).wait()

    wt = wt_ref[...]
    ffn = wt[:, 0:1] * r0_scr[slot] + wt[:, 1:2] * r1_scr[slot]
    xn = x_ref[...] + g2_ref[0] * ffn
    if final:
        ms = jnp.mean(xn * xn, axis=-1, keepdims=True)
        xn = xn * lax.rsqrt(ms + EPS) * fw_ref[...]
    o_ref[...] = xn


def _combine(dest, x2, wt, g2, fw, yb, seq, final):
    n, d = x2.shape
    tm = min(TOKEN_TILE, seq)
    tpb = seq // tm
    last = n // tm - 1
    return pl.pallas_call(
        functools.partial(_combine_kernel, final),
        grid=(n // tm,),
        in_specs=[
            pl.BlockSpec((2, tm), lambda i: (0, i), memory_space=pltpu.SMEM),
            pl.BlockSpec((2, tm), lambda i: (0, jnp.minimum(i + 1, last)), memory_space=pltpu.SMEM),
            pl.BlockSpec((tm, d), lambda i: (i, 0)),
            pl.BlockSpec((tm, 2), lambda i: (i, 0)),
            pl.BlockSpec((1, 1, d), lambda i: (i // tpb, 0, 0)),
            pl.BlockSpec((1, d), lambda i: (0, 0)),
            pl.BlockSpec(memory_space=pl.ANY),
        ],
        out_specs=pl.BlockSpec((tm, d), lambda i: (i, 0)),
        out_shape=jax.ShapeDtypeStruct((n, d), F32),
        scratch_shapes=[
            pltpu.VMEM((2, tm, d), F32),
            pltpu.VMEM((2, tm, d), F32),
            pltpu.SemaphoreType.DMA((2,)),
        ],
        compiler_params=_cparams(("arbitrary",)),
        name="moe_combine",
    )(dest, dest, x2, wt, g2, fw, yb)


def _layer_params(l, d, w_in, conv_w, conv_b, dt_bias, a_log, d_skip, ssm_norm_w,
                  lb_all, hgrn_norm_w):
    gn = SSM_GROUPS * SSM_STATE
    n_heads = d // SSM_HEAD_DIM
    w_r, w_dt = _relayout_w_in(w_in, l)
    pad_h = lambda v: jnp.pad(v.astype(F32), (0, LANES - n_heads)).reshape(1, LANES)
    lb = jnp.maximum(lb_all[l], 0.0)
    head_of_col = jnp.arange(d, dtype=jnp.int32) // SSM_HEAD_DIM
    head_expand = (jnp.arange(LANES, dtype=jnp.int32)[:, None] == head_of_col[None, :]).astype(BF16)
    cw, cb = conv_w[l], conv_b[l]
    return dict(
        w_r=w_r, w_dt=w_dt,
        cwx=cw[:, :d], cwb=cw[:, d:d + gn], cwc=cw[:, d + gn:],
        cbx=cb[:d].reshape(1, d), cbb=cb[d:d + gn].reshape(1, gn), cbc=cb[d + gn:].reshape(1, gn),
        dt_bias=pad_h(dt_bias[l]), a_log=pad_h(a_log[l]),
        d_skip=jnp.repeat(d_skip[l].astype(F32), SSM_HEAD_DIM).reshape(1, d),
        ssm_norm_w=ssm_norm_w[l].reshape(1, d),
        head_expand=head_expand,
        log_lb=(jnp.log(lb) * LOG2E).reshape(1, d), log_1m_lb=(jnp.log1p(-lb) * LOG2E).reshape(1, d),
        hgrn_norm_w=hgrn_norm_w[l].reshape(1, d),
    )


def kernel(x, c, ada_w, ada_b, norm1_w, norm2_w, w_in, conv_w, conv_b, dt_bias, a_log, d_skip,
           ssm_norm_w, hgrn_lower_bounds, hgrn_norm_w, w_out, router_w, router_bias,
           w_gate, w_up, w_down, final_norm_w):
    bsz, seq, d = x.shape
    n_layers = ada_w.shape[0]
    n_exp = router_w.shape[1]
    n = bsz * seq
    n_assign = 2 * n
    nb = n_assign // MOE_ROWS + n_exp
    n_rows = nb * MOE_ROWS

    lb_all = jnp.cumsum(jax.nn.softmax(hgrn_lower_bounds.astype(F32), axis=0), axis=0)
    lb_all = lb_all - lb_all[0:1]
    mod = _modulation(c, ada_w, ada_b)
    rwt = router_w.astype(F32).T
    rb = router_bias.astype(F32).reshape(n_exp, 1)
    fw = final_norm_w.reshape(1, d)

    x2 = x.reshape(n, d)
    for l in range(n_layers):
        sh1, sc1, g1, sh2, sc2, g2 = [mod[l, :, k * d:(k + 1) * d].reshape(bsz, 1, d) for k in range(6)]
        p = _layer_params(l, d, w_in, conv_w, conv_b, dt_bias, a_log, d_skip, ssm_norm_w,
                          lb_all, hgrn_norm_w)
        proj, dt_raw = _in_projection(x2, norm1_w[l].reshape(1, d), sc1, sh1, p["w_r"], p["w_dt"], seq)
        ya = _ssd_branch(proj, dt_raw, p, bsz, seq, d)
        yb = _hgrn_branch(proj, p, bsz, seq, d)
        x2, h2, mi, mw, cnt = _out_projection(
            x2, ya, yb, proj, w_out[l].astype(BF16), g1, norm2_w[l].reshape(1, d), sc2, sh2, rwt, rb, seq)

        counts = cnt[:, 0].astype(jnp.int32)
        padded = (counts + MOE_ROWS - 1) // MOE_ROWS * MOE_ROWS
        pend = jnp.cumsum(padded)
        pstart = pend - padded
        is_e = mi[0:2, :, None] == jnp.arange(n_exp, dtype=jnp.int32)
        dest = jnp.sum(jnp.where(is_e, pstart, 0), axis=-1) + mi[2:4]
        block_row0 = jnp.arange(nb, dtype=jnp.int32) * MOE_ROWS
        block_e = jnp.minimum(
            jnp.sum((pend[None, :] <= block_row0[:, None]).astype(jnp.int32), axis=1), n_exp - 1)
        n_used = (pend[-1:] // MOE_ROWS).astype(jnp.int32)

        xb = _dispatch(pend.astype(jnp.int32), dest, h2, n_rows)
        yexp = _experts(l, block_e, n_used, xb, w_gate, w_up, w_down)
        x2 = _combine(dest, x2, mw.T, g2, fw, yexp, seq, final=(l == n_layers - 1))
    return x2.reshape(bsz, seq, d)
```

```python
import functools

import jax
import jax.numpy as jnp
from jax import lax
from jax.experimental import pallas as pl
from jax.experimental.pallas import tpu as pltpu

F32 = jnp.float32
BF16 = jnp.bfloat16
EPS = 1e-6
LOG2E = 1.4426950408889634

SSM_HEAD_DIM = 64
SSM_STATE = 128
SSM_GROUPS = 4
CONV_K = 4
HGRN_HEAD = 128
N_EXPERT_GROUPS = 4
EXPERTS_PER_GROUP = 8
LANES = 128
SUBLANES = 8

SSD_CHUNK = 128
HGRN_CHUNK = 128
HGRN_HEADS_PER_STEP = 8
HGRN_CHUNKS_PER_STEP = 2
MOE_ROWS = 256
W_CHUNKS = 4
INPROJ_TM = 1024
INPROJ_TN = 1024
OUT_TM = 256
ROUTE_ROWS = 128
TOKEN_TILE = 256
DMA_UNROLL = 8
VMEM_LIMIT = 56 * 1024 * 1024


def _cparams(sem):
    return pltpu.CompilerParams(dimension_semantics=sem, vmem_limit_bytes=VMEM_LIMIT)


def _sigmoid(x):
    return 1.0 / (1.0 + jnp.exp(-x))


def _silu(x):
    return x * _sigmoid(x)


def _split3(x):
    hi = x.astype(BF16)
    r1 = x - hi.astype(F32)
    mid = r1.astype(BF16)
    lo = (r1 - mid.astype(F32)).astype(BF16)
    return hi, mid, lo


def _dot(a, b):
    return jnp.dot(a, b, preferred_element_type=F32)


def _dot_nt(a, b):
    return lax.dot_general(a, b, (((1,), (1,)), ((), ())), preferred_element_type=F32)


def _sel_dot(sel_bf16, x):
    hi, mid, lo = _split3(x)
    return _dot(sel_bf16, hi) + (_dot(sel_bf16, mid) + _dot(sel_bf16, lo))


def _dot_sel(x, sel_bf16):
    hi, mid, lo = _split3(x)
    return _dot(hi, sel_bf16) + (_dot(mid, sel_bf16) + _dot(lo, sel_bf16))


def _tril_ones(n):
    r = lax.broadcasted_iota(jnp.int32, (n, n), 0)
    c = lax.broadcasted_iota(jnp.int32, (n, n), 1)
    return jnp.where(r >= c, 1.0, 0.0).astype(BF16)


def _mod_kernel(c_ref, w_ref, b_ref, o_ref):
    ca = _silu(c_ref[...]).astype(BF16)
    o_ref[0] = _dot(ca, w_ref[0].astype(BF16)) + b_ref[0]


def _modulation(c, ada_w, ada_b):
    n_layers, d, n6 = ada_w.shape
    bsz = c.shape[0]
    tn = 1024
    return pl.pallas_call(
        _mod_kernel,
        grid=(n_layers, n6 // tn),
        in_specs=[
            pl.BlockSpec((bsz, d), lambda l, j: (0, 0)),
            pl.BlockSpec((1, d, tn), lambda l, j: (l, 0, j)),
            pl.BlockSpec((1, 1, tn), lambda l, j: (l, 0, j)),
        ],
        out_specs=pl.BlockSpec((1, bsz, tn), lambda l, j: (l, 0, j)),
        out_shape=jax.ShapeDtypeStruct((n_layers, bsz, n6), F32),
        compiler_params=_cparams(("arbitrary", "arbitrary")),
        name="adaln_mod",
    )(c, ada_w, ada_b.reshape(n_layers, 1, n6))


_DIRECT, _LOAD_ONLY, _SHIFTED = 0, 1, 2


def _relayout_kernel(shift, src_ref, dst_ref, mode_ref, w_ref, o_ref, odt_ref, prev_scr):
    del src_ref, dst_ref
    t = pl.program_id(0)
    mode = mode_ref[t]

    @pl.when(mode == _DIRECT)
    def _():
        o_ref[...] = w_ref[0].astype(BF16)

    @pl.when(mode == _LOAD_ONLY)
    def _():
        head = w_ref[0, :, :LANES]
        lane = lax.broadcasted_iota(jnp.int32, head.shape, 1)
        odt_ref[...] = jnp.where(lane < shift, head, 0.0).astype(BF16)
        prev_scr[...] = w_ref[0]

    @pl.when(mode == _SHIFTED)
    def _():
        o_ref[...] = jnp.concatenate([prev_scr[:, shift:], w_ref[0, :, :shift]], axis=1).astype(BF16)
        prev_scr[...] = w_ref[0]


def _relayout_w_in(w_in, layer):
    d = w_in.shape[1]
    gn2 = 2 * SSM_GROUPS * SSM_STATE
    n_heads = d // SSM_HEAD_DIM
    tn = INPROJ_TN
    assert gn2 == tn and (2 * d) % tn == 0 and 0 < n_heads < LANES
    n_front, n_rest = 2 * d // tn, 6 * d // tn
    src = list(range(n_front)) + [n_front + 1 + k for k in range(n_rest + 1)] + [n_front]
    dst = list(range(n_front)) + [n_front - 1] + [n_front + k for k in range(n_rest)] + [n_front + n_rest]
    mode = [_DIRECT] * n_front + [_LOAD_ONLY] + [_SHIFTED] * n_rest + [_DIRECT]
    n_tiles = n_front + n_rest + 1
    grid_spec = pltpu.PrefetchScalarGridSpec(
        num_scalar_prefetch=3,
        grid=(len(src),),
        in_specs=[pl.BlockSpec((1, d, tn), lambda t, s, o, m: (layer, 0, s[t]))],
        out_specs=[
            pl.BlockSpec((d, tn), lambda t, s, o, m: (0, o[t])),
            pl.BlockSpec((d, LANES), lambda t, s, o, m: (0, 0)),
        ],
        scratch_shapes=[pltpu.VMEM((d, tn), F32)],
    )
    return pl.pallas_call(
        functools.partial(_relayout_kernel, n_heads),
        grid_spec=grid_spec,
        out_shape=[
            jax.ShapeDtypeStruct((d, n_tiles * tn), BF16),
            jax.ShapeDtypeStruct((d, LANES), BF16),
        ],
        compiler_params=_cparams(("arbitrary",)),
        name="w_in_relayout",
    )(jnp.asarray(src, jnp.int32), jnp.asarray(dst, jnp.int32), jnp.asarray(mode, jnp.int32), w_in)


def _rms_mod(x, nw, sc, sh):
    ms = jnp.mean(x * x, axis=-1, keepdims=True)
    return (x * lax.rsqrt(ms + EPS) * nw) * (1.0 + sc) + sh


def _inproj_kernel(x_ref, nw_ref, sc_ref, sh_ref, w_ref, wdt_ref, o_ref, dt_ref, h_scr):
    @pl.when(pl.program_id(1) == 0)
    def _():
        h = _rms_mod(x_ref[...], nw_ref[...], sc_ref[0], sh_ref[0]).astype(BF16)
        h_scr[...] = h
        dt_ref[...] = _dot(h, wdt_ref[...])

    o_ref[...] = _dot(h_scr[...], w_ref[...])


def _in_projection(x2, nw, sc, sh, w_r, w_dt, seq):
    n, d = x2.shape
    np_ = w_r.shape[1]
    tm = min(INPROJ_TM, seq)
    tn = INPROJ_TN
    tiles_per_batch = seq // tm
    return pl.pallas_call(
        _inproj_kernel,
        grid=(n // tm, np_ // tn),
        in_specs=[
            pl.BlockSpec((tm, d), lambda i, j: (i, 0)),
            pl.BlockSpec((1, d), lambda i, j: (0, 0)),
            pl.BlockSpec((1, 1, d), lambda i, j: (i // tiles_per_batch, 0, 0)),
            pl.BlockSpec((1, 1, d), lambda i, j: (i // tiles_per_batch, 0, 0)),
            pl.BlockSpec((d, tn), lambda i, j: (0, j)),
            pl.BlockSpec((d, LANES), lambda i, j: (0, 0)),
        ],
        out_specs=[
            pl.BlockSpec((tm, tn), lambda i, j: (i, j)),
            pl.BlockSpec((tm, LANES), lambda i, j: (i, 0)),
        ],
        out_shape=[
            jax.ShapeDtypeStruct((n, np_), F32),
            jax.ShapeDtypeStruct((n, LANES), F32),
        ],
        scratch_shapes=[pltpu.VMEM((tm, d), BF16)],
        compiler_params=_cparams(("arbitrary", "arbitrary")),
        name="norm_inproj",
    )(x2, nw, sc, sh, w_r, w_dt)


def _causal_conv(ext_ref, u, w_ref, b_ref, first):
    q = u.shape[0]

    @pl.when(first)
    def _():
        ext_ref[0:SUBLANES, :] = jnp.zeros((SUBLANES, u.shape[1]), F32)

    ext_ref[SUBLANES:SUBLANES + q, :] = u
    acc = b_ref[...] + w_ref[CONV_K - 1:CONV_K, :] * u
    for k in range(CONV_K - 1):
        off = SUBLANES - (CONV_K - 1) + k
        acc = acc + w_ref[k:k + 1, :] * ext_ref[off:off + q, :]
    ext_ref[0:SUBLANES, :] = ext_ref[q:q + SUBLANES, :]
    return _silu(acc)


def _ssd_kernel(z_ref, xs_ref, bm_ref, cm_ref, dt_ref,
                cwx_ref, cwb_ref, cwc_ref, cbx_ref, cbb_ref, cbc_ref,
                dtb_ref, alog_ref, dskip_ref, nw_ref, exp_ref,
                o_ref,
                state_scr, extx_scr, extb_scr, extc_scr):
    q = z_ref.shape[0]
    d_inner = z_ref.shape[1]
    gw = d_inner // SSM_GROUPS
    heads_per_group = gw // SSM_HEAD_DIM
    first = pl.program_id(1) == 0

    @pl.when(first)
    def _():
        state_scr[...] = jnp.zeros(state_scr.shape, F32)

    xs = _causal_conv(extx_scr, xs_ref[...], cwx_ref, cbx_ref, first)
    bmat = _causal_conv(extb_scr, bm_ref[...], cwb_ref, cbb_ref, first)
    cmat = _causal_conv(extc_scr, cm_ref[...], cwc_ref, cbc_ref, first)

    x_dt = dt_ref[...] + dtb_ref[...]
    dt = jnp.maximum(x_dt, 0.0) + jnp.log(1.0 + jnp.exp(-jnp.abs(x_dt)))
    a = -jnp.exp(alog_ref[...])
    acs = _sel_dot(_tril_ones(q), dt * a)
    acs_last = acs[q - 1:q, :]
    stack = jnp.concatenate(
        [dt, jnp.exp(acs), jnp.exp(acs_last - acs),
         jnp.broadcast_to(jnp.exp(acs_last), (SUBLANES, LANES))], axis=0)
    wide = _dot_sel(stack, exp_ref[...])
    dt_e = wide[0:q]
    eacs_e = wide[q:2 * q]
    dec_e = wide[2 * q:3 * q]
    cd_e = wide[3 * q:3 * q + 1]

    xdt = xs * dt_e
    xdt_b = xdt.astype(BF16)
    xdec_b = (xdt * dec_e).astype(BF16)

    acs_t = acs.T
    ri = lax.broadcasted_iota(jnp.int32, (q, q), 0)
    ci = lax.broadcasted_iota(jnp.int32, (q, q), 1)
    causal = ri >= ci
    lane = lax.broadcasted_iota(jnp.int32, (q, LANES), 1)
    lo_half = lane < SSM_HEAD_DIM

    y_parts = []
    for g in range(SSM_GROUPS):
        bg = bmat[:, g * SSM_STATE:(g + 1) * SSM_STATE]
        cg = cmat[:, g * SSM_STATE:(g + 1) * SSM_STATE]
        bg_b = bg.astype(BF16)
        cg_b = cg.astype(BF16)
        cb = _dot_nt(cg_b, bg_b)
        st = state_scr[g]
        y_off = _dot(cg_b, st.astype(BF16)) * eacs_e[:, g * gw:(g + 1) * gw]
        diag = []
        for pair in range(heads_per_group // 2):
            ms = []
            for sub in range(2):
                h = g * heads_per_group + pair * 2 + sub
                rel = acs[:, h:h + 1] - acs_t[h:h + 1, :]
                lmat = jnp.exp(jnp.where(causal, rel, -jnp.inf))
                ms.append((cb * lmat).astype(BF16))
            lhs = jnp.concatenate(ms, axis=1)
            c0 = (g * heads_per_group + pair * 2) * SSM_HEAD_DIM
            xp = xdt_b[:, c0:c0 + LANES]
            zero = jnp.zeros_like(xp)
            rhs = jnp.concatenate(
                [jnp.where(lo_half, xp, zero), jnp.where(lo_half, zero, xp)], axis=0)
            diag.append(_dot(lhs, rhs))
        y_parts.append(jnp.concatenate(diag, axis=1) + y_off)
        new_state = st * cd_e[:, g * gw:(g + 1) * gw] + _dot(
            bg.T.astype(BF16), xdec_b[:, g * gw:(g + 1) * gw])
        state_scr[g] = new_state

    y = jnp.concatenate(y_parts, axis=1) + dskip_ref[...] * xs
    y = y * _silu(z_ref[...])
    outs = []
    for g in range(SSM_GROUPS):
        yg = y[:, g * gw:(g + 1) * gw]
        ms = jnp.mean(yg * yg, axis=-1, keepdims=True)
        outs.append(yg * lax.rsqrt(ms + EPS))
    o_ref[...] = (jnp.concatenate(outs, axis=1) * nw_ref[...]).astype(o_ref.dtype)


def _ssd_branch(proj, dt_raw, p, bsz, seq, d):
    q = min(SSD_CHUNK, seq)
    nc = seq // q
    gn = SSM_GROUPS * SSM_STATE
    b_blk = 8 * d // gn
    row = lambda b, c: b * nc + c
    vec = lambda width: pl.BlockSpec((1, width), lambda b, c: (0, 0))
    return pl.pallas_call(
        _ssd_kernel,
        grid=(bsz, nc),
        in_specs=[
            pl.BlockSpec((q, d), lambda b, c: (row(b, c), 0)),
            pl.BlockSpec((q, d), lambda b, c: (row(b, c), 1)),
            pl.BlockSpec((q, gn), lambda b, c: (row(b, c), b_blk)),
            pl.BlockSpec((q, gn), lambda b, c: (row(b, c), b_blk + 1)),
            pl.BlockSpec((q, LANES), lambda b, c: (row(b, c), 0)),
            pl.BlockSpec((CONV_K, d), lambda b, c: (0, 0)),
            pl.BlockSpec((CONV_K, gn), lambda b, c: (0, 0)),
            pl.BlockSpec((CONV_K, gn), lambda b, c: (0, 0)),
            vec(d), vec(gn), vec(gn),
            vec(LANES), vec(LANES), vec(d), vec(d),
            pl.BlockSpec((LANES, d), lambda b, c: (0, 0)),
        ],
        out_specs=pl.BlockSpec((q, d), lambda b, c: (row(b, c), 0)),
        out_shape=jax.ShapeDtypeStruct((bsz * seq, d), BF16),
        scratch_shapes=[
            pltpu.VMEM((SSM_GROUPS, SSM_STATE, d // SSM_GROUPS), F32),
            pltpu.VMEM((q + SUBLANES, d), F32),
            pltpu.VMEM((q + SUBLANES, gn), F32),
            pltpu.VMEM((q + SUBLANES, gn), F32),
        ],
        compiler_params=_cparams(("arbitrary", "arbitrary")),
        name="ssd_scan",
    )(proj, proj, proj, proj, dt_raw,
      p["cwx"], p["cwb"], p["cwc"], p["cbx"], p["cbb"], p["cbc"],
      p["dt_bias"], p["a_log"], p["d_skip"], p["ssm_norm_w"], p["head_expand"])


def _hgrn_chunk(q_raw, f_raw, v, g_raw, loglb, log1mlb, nw, s_t, tril, xor_idx, below):
    c = q_raw.shape[0]
    nv = c // SUBLANES
    f2 = f_raw * LOG2E
    logsig = jnp.minimum(f2, 0.0) - jnp.log2(1.0 + jnp.exp2(-jnp.abs(f2)))
    bb = log1mlb + logsig
    logf2 = jnp.maximum(loglb, bb) + jnp.log2(1.0 + jnp.exp2(-jnp.abs(loglb - bb)))
    kk = jnp.exp2(bb - f2)
    qf = _silu(q_raw)
    b = _sel_dot(tril, logf2)

    b3 = b.reshape(nv, SUBLANES, LANES)
    q3 = qf.reshape(nv, SUBLANES, LANES)
    k3 = kk.reshape(nv, SUBLANES, LANES)
    sub = lax.broadcasted_iota(jnp.int32, (1, SUBLANES, LANES), 1)

    def brow(i):
        return jnp.broadcast_to(b3[:, i:i + 1, :], b3.shape)

    levels = []
    up = (sub & 1) != 0
    levels.append((1, jnp.where(up, q3 * jnp.exp2(logf2.reshape(b3.shape)), k3).reshape(c, LANES)))
    up = (sub & 2) != 0
    d = (b3 - jnp.where(sub < 4, brow(1), brow(5))) * jnp.where(up, 1.0, -1.0)
    levels.append((2, (jnp.where(up, q3, k3) * jnp.exp2(d)).reshape(c, LANES)))
    up = sub >= 4
    d = (b3 - brow(3)) * jnp.where(up, 1.0, -1.0)
    levels.append((4, (jnp.where(up, q3, k3) * jnp.exp2(d)).reshape(c, LANES)))
    w = SUBLANES
    while w < c:
        parts = []
        for j in range(c // (2 * w)):
            lo, mid, hi = 2 * w * j, 2 * w * j + w, 2 * w * (j + 1)
            bm = b[mid - 1:mid, :]
            parts.append(kk[lo:mid] * jnp.exp2(bm - b[lo:mid]))
            parts.append(qf[mid:hi] * jnp.exp2(b[mid:hi] - bm))
        levels.append((w, jnp.concatenate(parts, axis=0)))
        w *= 2

    scores = None
    for w, xw in reversed(levels):
        xb = xw.astype(BF16)
        a_w = _dot_nt(xb, xb)
        scores = a_w if scores is None else jnp.where(xor_idx < 2 * w, a_w, scores)
    scores = jnp.where(below, scores, 0.0)

    o = _dot(scores.astype(BF16), v.astype(BF16))
    o = o + jnp.sum(qf * kk, axis=-1, keepdims=True) * v
    o = o + _dot_nt((qf * jnp.exp2(b)).astype(BF16), s_t.astype(BF16))
    b_last = b[c - 1:c, :]
    kst = (kk * jnp.exp2(b_last - b)).astype(BF16)
    s_new = s_t * jnp.exp2(b_last) + _dot(v.T.astype(BF16), kst)

    ms = jnp.mean(o * o, axis=-1, keepdims=True)
    y = o * lax.rsqrt(ms + EPS) * nw * _silu(g_raw)
    return y, s_new


def _hgrn_kernel(q_ref, f_ref, i_ref, g_ref, loglb_ref, log1mlb_ref, nw_ref, o_ref, state_scr):
    rows = q_ref.shape[0]
    heads = q_ref.shape[1] // HGRN_HEAD
    c = min(HGRN_CHUNK, rows)

    @pl.when(pl.program_id(2) == 0)
    def _():
        state_scr[...] = jnp.zeros(state_scr.shape, F32)

    tril = _tril_ones(c)
    ri = lax.broadcasted_iota(jnp.int32, (c, c), 0)
    ci = lax.broadcasted_iota(jnp.int32, (c, c), 1)
    xor_idx = ri ^ ci
    below = ri > ci
    for h in range(heads):
        cols = slice(h * HGRN_HEAD, (h + 1) * HGRN_HEAD)
        s_t = state_scr[h]
        for k in range(rows // c):
            rs = slice(k * c, (k + 1) * c)
            y, s_t = _hgrn_chunk(q_ref[rs, cols], f_ref[rs, cols], i_ref[rs, cols], g_ref[rs, cols],
                                 loglb_ref[:, cols], log1mlb_ref[:, cols], nw_ref[:, cols], s_t,
                                 tril, xor_idx, below)
            o_ref[rs, cols] = y.astype(o_ref.dtype)
        state_scr[h] = s_t


def _hgrn_branch(proj, p, bsz, seq, d):
    n_heads = d // HGRN_HEAD
    hb = min(HGRN_HEADS_PER_STEP, n_heads)
    wblk = hb * HGRN_HEAD
    rows = min(HGRN_CHUNK * HGRN_CHUNKS_PER_STEP, seq)
    nt = seq // rows
    per = d // wblk

    def sect(k):
        return pl.BlockSpec((rows, wblk), lambda b, hg, t: (b * nt + t, k * per + hg))

    vec = pl.BlockSpec((1, wblk), lambda b, hg, t: (0, hg))
    return pl.pallas_call(
        _hgrn_kernel,
        grid=(bsz, n_heads // hb, nt),
        in_specs=[sect(2), sect(3), sect(4), sect(5), vec, vec, vec],
        out_specs=pl.BlockSpec((rows, wblk), lambda b, hg, t: (b * nt + t, hg)),
        out_shape=jax.ShapeDtypeStruct((bsz * seq, d), BF16),
        scratch_shapes=[pltpu.VMEM((hb, HGRN_HEAD, HGRN_HEAD), F32)],
        compiler_params=_cparams(("arbitrary", "arbitrary", "arbitrary")),
        name="hgrn2_scan",
    )(proj, proj, proj, proj, p["log_lb"], p["log_1m_lb"], p["hgrn_norm_w"])


def _first_argmax(vals, idx, n):
    m = jnp.max(vals, axis=0, keepdims=True)
    i = jnp.min(jnp.where(vals == m, idx, float(n)), axis=0, keepdims=True)
    return m, i


def _outproj_kernel(x_ref, ya_ref, yb_ref, ga_ref, gb_ref, wout_ref, g1_ref,
                    nw2_ref, sc2_ref, sh2_ref, rwt_ref, rb_ref,
                    xo_ref, h2_ref, mi_ref, mw_ref, cnt_ref, cnt_scr):
    tm = x_ref.shape[0]

    @pl.when(pl.program_id(0) == 0)
    def _():
        cnt_scr[...] = jnp.zeros(cnt_scr.shape, F32)

    r_hi, r_mid, _ = _split3(rwt_ref[...])
    ts = min(ROUTE_ROWS, tm)
    total = cnt_scr[...]
    for s in range(tm // ts):
        total = _outproj_rows(slice(s * ts, (s + 1) * ts), total, (r_hi, r_mid),
                              x_ref, ya_ref, yb_ref, ga_ref, gb_ref, wout_ref, g1_ref,
                              nw2_ref, sc2_ref, sh2_ref, rb_ref, xo_ref, h2_ref, mi_ref, mw_ref)
    cnt_scr[...] = total
    cnt_ref[...] = jnp.broadcast_to(total, cnt_ref.shape)


def _outproj_rows(rs, base, r_split, x_ref, ya_ref, yb_ref, ga_ref, gb_ref, wout_ref, g1_ref,
                  nw2_ref, sc2_ref, sh2_ref, rb_ref, xo_ref, h2_ref, mi_ref, mw_ref):
    tm = rs.stop - rs.start
    n_exp = rb_ref.shape[0]
    r_hi, r_mid = r_split
    merged = (_sigmoid(ga_ref[rs, :]) * ya_ref[rs, :].astype(F32)
              + _sigmoid(gb_ref[rs, :]) * yb_ref[rs, :].astype(F32))
    mix = _dot(merged.astype(BF16), wout_ref[...])
    xn = x_ref[rs, :] + g1_ref[0] * mix
    xo_ref[rs, :] = xn
    h2 = _rms_mod(xn, nw2_ref[...], sc2_ref[0], sh2_ref[0])
    h2_ref[rs, :] = h2

    h_hi, h_mid, _ = _split3(h2)
    logits = _dot_nt(r_hi, h_hi) + (_dot_nt(r_hi, h_mid) + _dot_nt(r_mid, h_hi))
    scores = _sigmoid(logits)
    sel = scores + rb_ref[...]

    sub = lax.broadcasted_iota(jnp.int32, (EXPERTS_PER_GROUP, tm), 0).astype(F32)
    neg = -jnp.inf
    best = None
    for g in range(N_EXPERT_GROUPS):
        blk = sel[g * EXPERTS_PER_GROUP:(g + 1) * EXPERTS_PER_GROUP]
        m1, i1 = _first_argmax(blk, sub, EXPERTS_PER_GROUP)
        m2 = jnp.max(jnp.where(sub == i1, neg, blk), axis=0, keepdims=True)
        gs = m1 + m2
        if best is None:
            best, gi = gs, jnp.zeros_like(gs)
            in_sel, in_sc = blk, scores[0:EXPERTS_PER_GROUP]
        else:
            upd = gs > best
            best = jnp.where(upd, gs, best)
            gi = jnp.where(upd, float(g), gi)
            in_sel = jnp.where(upd, blk, in_sel)
            in_sc = jnp.where(upd, scores[g * EXPERTS_PER_GROUP:(g + 1) * EXPERTS_PER_GROUP], in_sc)
    _, l1 = _first_argmax(in_sel, sub, EXPERTS_PER_GROUP)
    _, l2 = _first_argmax(jnp.where(sub == l1, neg, in_sel), sub, EXPERTS_PER_GROUP)
    s1 = jnp.sum(jnp.where(sub == l1, in_sc, 0.0), axis=0, keepdims=True)
    s2 = jnp.sum(jnp.where(sub == l2, in_sc, 0.0), axis=0, keepdims=True)
    e1 = gi * EXPERTS_PER_GROUP + l1
    e2 = gi * EXPERTS_PER_GROUP + l2
    mw_ref[0:1, rs] = s1 / (s1 + s2)
    mw_ref[1:2, rs] = s2 / (s1 + s2)

    eio = lax.broadcasted_iota(jnp.int32, (n_exp, tm), 0).astype(F32)
    oh1 = eio == e1
    oh2 = eio == e2
    onehot = jnp.where(oh1 | oh2, 1.0, 0.0)
    tr = lax.broadcasted_iota(jnp.int32, (tm, tm), 0)
    tc = lax.broadcasted_iota(jnp.int32, (tm, tm), 1)
    before = jnp.where(tr < tc, 1.0, 0.0).astype(BF16)
    pref = _dot(onehot.astype(BF16), before) + base
    r1 = jnp.sum(jnp.where(oh1, pref, 0.0), axis=0, keepdims=True)
    r2 = jnp.sum(jnp.where(oh2, pref, 0.0), axis=0, keepdims=True)
    mi_ref[0:1, rs] = e1.astype(jnp.int32)
    mi_ref[1:2, rs] = e2.astype(jnp.int32)
    mi_ref[2:3, rs] = r1.astype(jnp.int32)
    mi_ref[3:4, rs] = r2.astype(jnp.int32)
    return base + jnp.sum(onehot, axis=1, keepdims=True)


def _out_projection(x2, ya, yb, proj, w_out_b, g1, nw2, sc2, sh2, rwt, rb, seq):
    n, d = x2.shape
    n_exp = rwt.shape[0]
    tm = min(OUT_TM, seq)
    tpb = seq // tm
    bvec = pl.BlockSpec((1, 1, d), lambda i: (i // tpb, 0, 0))
    return pl.pallas_call(
        _outproj_kernel,
        grid=(n // tm,),
        in_specs=[
            pl.BlockSpec((tm, d), lambda i: (i, 0)),
            pl.BlockSpec((tm, d), lambda i: (i, 0)),
            pl.BlockSpec((tm, d), lambda i: (i, 0)),
            pl.BlockSpec((tm, d), lambda i: (i, 6)),
            pl.BlockSpec((tm, d), lambda i: (i, 7)),
            pl.BlockSpec((d, d), lambda i: (0, 0)),
            bvec,
            pl.BlockSpec((1, d), lambda i: (0, 0)),
            bvec, bvec,
            pl.BlockSpec((n_exp, d), lambda i: (0, 0)),
            pl.BlockSpec((n_exp, 1), lambda i: (0, 0)),
        ],
        out_specs=[
            pl.BlockSpec((tm, d), lambda i: (i, 0)),
            pl.BlockSpec((tm, d), lambda i: (i, 0)),
            pl.BlockSpec((4, tm), lambda i: (0, i)),
            pl.BlockSpec((2, tm), lambda i: (0, i)),
            pl.BlockSpec((n_exp, LANES), lambda i: (0, 0)),
        ],
        out_shape=[
            jax.ShapeDtypeStruct((n, d), F32),
            jax.ShapeDtypeStruct((n, d), F32),
            jax.ShapeDtypeStruct((4, n), jnp.int32),
            jax.ShapeDtypeStruct((2, n), F32),
            jax.ShapeDtypeStruct((n_exp, LANES), F32),
        ],
        scratch_shapes=[pltpu.VMEM((n_exp, 1), F32)],
        compiler_params=_cparams(("arbitrary",)),
        name="merge_outproj_router",
    )(x2, ya, yb, proj, proj, w_out_b, g1, nw2, sc2, sh2, rwt, rb)


def _row_copy(src_ref, src_row, dst_ref, dst_row, sem):
    return pltpu.make_async_copy(src_ref.at[pl.ds(src_row, 1)], dst_ref.at[pl.ds(dst_row, 1)], sem)


def _dispatch_kernel(pend_ref, dest_ref, h_ref, xb_ref, zero_scr, sem, zero_sem):
    tm = h_ref.shape[0]
    n_exp = pend_ref.shape[0]

    @pl.when(pl.program_id(0) == 0)
    def _():
        zero_scr[...] = jnp.zeros(zero_scr.shape, F32)

        def clear(row0):
            row0 = pl.multiple_of(row0, MOE_ROWS)
            return pltpu.make_async_copy(zero_scr, xb_ref.at[pl.ds(row0, MOE_ROWS)], zero_sem)

        def last_block(e):
            return clear(jnp.maximum(pend_ref[e] - MOE_ROWS, 0))

        for e in range(n_exp):
            last_block(e).start()
        for e in range(n_exp):
            last_block(e).wait()

        n_used = pend_ref[n_exp - 1] // MOE_ROWS
        n_blocks = xb_ref.shape[0] // MOE_ROWS

        def start_tail(j, carry):
            clear(j * MOE_ROWS).start()
            return carry

        def wait_tail(j, carry):
            clear(j * MOE_ROWS).wait()
            return carry

        lax.fori_loop(n_used, n_blocks, start_tail, 0)
        lax.fori_loop(n_used, n_blocks, wait_tail, 0)

    def start(t, carry):
        _row_copy(h_ref, t, xb_ref, dest_ref[0, t], sem).start(priority=0)
        _row_copy(h_ref, t, xb_ref, dest_ref[1, t], sem).start(priority=1)
        return carry

    lax.fori_loop(0, tm, start, 0, unroll=DMA_UNROLL)
    for _ in range(2):
        pltpu.make_async_copy(h_ref, xb_ref.at[pl.ds(0, tm)], sem).wait()


def _dispatch(pend, dest, h2, n_rows):
    n, d = h2.shape
    tm = min(TOKEN_TILE, n)
    grid_spec = pltpu.PrefetchScalarGridSpec(
        num_scalar_prefetch=1,
        grid=(n // tm,),
        in_specs=[
            pl.BlockSpec((2, tm), lambda i, pe: (0, i), memory_space=pltpu.SMEM),
            pl.BlockSpec((tm, d), lambda i, pe: (i, 0)),
        ],
        out_specs=pl.BlockSpec(memory_space=pl.ANY),
        scratch_shapes=[
            pltpu.VMEM((MOE_ROWS, d), F32),
            pltpu.SemaphoreType.DMA(()),
            pltpu.SemaphoreType.DMA(()),
        ],
    )
    return pl.pallas_call(
        _dispatch_kernel,
        grid_spec=grid_spec,
        out_shape=jax.ShapeDtypeStruct((n_rows, d), F32),
        compiler_params=_cparams(("arbitrary",)),
        name="moe_dispatch",
    )(pend, dest, h2)


def _expert_schedule(padded, n_steps):
    n_exp = padded.shape[0]
    eids = jnp.arange(n_exp, dtype=jnp.int32)
    nblk = padded // MOE_ROWS
    blk0 = (jnp.cumsum(padded) - padded) // MOE_ROWS
    nonempty = nblk > 0
    rank = jnp.cumsum(nonempty.astype(jnp.int32)) - 1
    n_nonempty = rank[-1] + 1
    has_next = nonempty & (rank + 1 < n_nonempty)
    seg_len = jnp.where(nonempty, jnp.maximum(nblk, jnp.where(has_next, W_CHUNKS, 0)), 0)
    seg_end = W_CHUNKS + jnp.cumsum(seg_len)
    seg_start = seg_end - seg_len
    total = seg_end[-1]
    by_rank = jnp.sum(jnp.where(nonempty[None, :] & (rank[None, :] == eids[:, None]), eids[None, :], 0), axis=1)

    s = jnp.arange(n_steps, dtype=jnp.int32)
    in_prologue = s < W_CHUNKS
    active = (s >= W_CHUNKS) & (s < total)
    e = jnp.minimum(jnp.sum((seg_end[None, :] <= s[:, None]).astype(jnp.int32), axis=1), n_exp - 1)
    pick = lambda v: jnp.sum(jnp.where(eids[None, :] == e[:, None], v[None, :], 0), axis=1)
    off = s - pick(seg_start)
    nblk_s, len_s = pick(nblk), pick(seg_len)
    do_compute = active & (off < nblk_s)
    last_block = (total > W_CHUNKS) * (jnp.sum(nblk) - 1)
    x_block = jnp.where(active, pick(blk0) + jnp.minimum(off, nblk_s - 1),
                        jnp.where(in_prologue, 0, last_block))
    do_convert = in_prologue | (active & (pick(has_next.astype(jnp.int32)) > 0) & (off >= len_s - W_CHUNKS))
    conv_no = jnp.maximum(jnp.cumsum(do_convert.astype(jnp.int32)) - 1, 0)
    conv_rank = conv_no // W_CHUNKS
    conv_expert = jnp.sum(jnp.where(eids[None, :] == conv_rank[:, None], by_rank[None, :], 0), axis=1)
    as_i32 = lambda v: v.astype(jnp.int32)
    return (as_i32(x_block), as_i32(do_compute), as_i32(pick(rank) % 2), as_i32(do_convert),
            as_i32(conv_expert), as_i32(conv_no % W_CHUNKS), as_i32(conv_rank % 2))


def _expert_kernel(xblk_ref, compute_ref, cslot_ref, convert_ref, cexp_ref, chunk_ref, vslot_ref,
                   x_ref, wgc_ref, wuc_ref, wdc_ref, o_ref, wg_scr, wu_scr, wd_scr):
    del xblk_ref, cexp_ref
    s = pl.program_id(0)
    ch = wgc_ref.shape[3]

    @pl.when(convert_ref[s] != 0)
    def _():
        slot = vslot_ref[s]
        for c in range(W_CHUNKS):
            @pl.when(chunk_ref[s] == c)
            def _():
                wg_scr[slot, :, c * ch:(c + 1) * ch] = wgc_ref[0, 0].astype(BF16)
                wu_scr[slot, :, c * ch:(c + 1) * ch] = wuc_ref[0, 0].astype(BF16)
                wd_scr[slot, c * ch:(c + 1) * ch, :] = wdc_ref[0, 0].astype(BF16)

    @pl.when(compute_ref[s] != 0)
    def _():
        slot = cslot_ref[s]
        x = x_ref[...].astype(BF16)
        hid = _silu(_dot(x, wg_scr[slot])) * _dot(x, wu_scr[slot])
        o_ref[...] = _dot(hid.astype(BF16), wd_scr[slot])


def _experts(layer, padded, xb, w_gate, w_up, w_down):
    n_rows, d = xb.shape
    n_exp, f = w_gate.shape[1], w_gate.shape[3]
    ch = f // W_CHUNKS
    n_steps = n_rows // MOE_ROWS + W_CHUNKS * (n_exp + 1)
    tables = _expert_schedule(padded, n_steps)
    n_tab = len(tables)
    grid_spec = pltpu.PrefetchScalarGridSpec(
        num_scalar_prefetch=n_tab,
        grid=(n_steps,),
        in_specs=[
            pl.BlockSpec((MOE_ROWS, d), lambda s, xb_, cp, cs, cv, ce, ck, vs: (xb_[s], 0)),
            pl.BlockSpec((1, 1, d, ch), lambda s, xb_, cp, cs, cv, ce, ck, vs: (layer, ce[s], 0, ck[s])),
            pl.BlockSpec((1, 1, d, ch), lambda s, xb_, cp, cs, cv, ce, ck, vs: (layer, ce[s], 0, ck[s])),
            pl.BlockSpec((1, 1, ch, d), lambda s, xb_, cp, cs, cv, ce, ck, vs: (layer, ce[s], ck[s], 0)),
        ],
        out_specs=pl.BlockSpec((MOE_ROWS, d), lambda s, xb_, cp, cs, cv, ce, ck, vs: (xb_[s], 0)),
        scratch_shapes=[
            pltpu.VMEM((2, d, f), BF16),
            pltpu.VMEM((2, d, f), BF16),
            pltpu.VMEM((2, f, d), BF16),
        ],
    )
    return pl.pallas_call(
        _expert_kernel,
        grid_spec=grid_spec,
        out_shape=jax.ShapeDtypeStruct((n_rows, d), F32),
        input_output_aliases={n_tab: 0},
        compiler_params=_cparams(("arbitrary",)),
        name="moe_experts",
    )(*tables, xb, w_gate, w_up, w_down)


def _combine_kernel(final, dest_ref, dest_next_ref, x_ref, wt_ref, g2_ref, fw_ref, yb_ref, o_ref,
                    r0_scr, r1_scr, sems):
    tm = x_ref.shape[0]
    i = pl.program_id(0)
    slot = i % 2

    def gather(idx_ref, s):
        def start(t, carry):
            _row_copy(yb_ref, idx_ref[0, t], r0_scr.at[s], t, sems.at[s]).start(priority=0)
            _row_copy(yb_ref, idx_ref[1, t], r1_scr.at[s], t, sems.at[s]).start(priority=1)
            return carry

        lax.fori_loop(0, tm, start, 0, unroll=DMA_UNROLL)

    @pl.when(i == 0)
    def _():
        gather(dest_ref, slot)

    @pl.when(i + 1 < pl.num_programs(0))
    def _():
        gather(dest_next_ref, 1 - slot)

    pltpu.make_async_copy(yb_ref.at[pl.ds(0, tm)], r0_scr.at[slot], sems.at[slot]).wait()
    pltpu.make_async_copy(yb_ref.at[pl.ds(0, tm)], r1_scr.at[slot], sems.at[slot]).wait()

    wt = wt_ref[...]
    ffn = wt[:, 0:1] * r0_scr[slot] + wt[:, 1:2] * r1_scr[slot]
    xn = x_ref[...] + g2_ref[0] * ffn
    if final:
        ms = jnp.mean(xn * xn, axis=-1, keepdims=True)
        xn = xn * lax.rsqrt(ms + EPS) * fw_ref[...]
    o_ref[...] = xn


def _combine(dest, x2, wt, g2, fw, yb, seq, final):
    n, d = x2.shape
    tm = min(TOKEN_TILE, seq)
    tpb = seq // tm
    last = n // tm - 1
    return pl.pallas_call(
        functools.partial(_combine_kernel, final),
        grid=(n // tm,),
        in_specs=[
            pl.BlockSpec((2, tm), lambda i: (0, i), memory_space=pltpu.SMEM),
            pl.BlockSpec((2, tm), lambda i: (0, jnp.minimum(i + 1, last)), memory_space=pltpu.SMEM),
            pl.BlockSpec((tm, d), lambda i: (i, 0)),
            pl.BlockSpec((tm, 2), lambda i: (i, 0)),
            pl.BlockSpec((1, 1, d), lambda i: (i // tpb, 0, 0)),
            pl.BlockSpec((1, d), lambda i: (0, 0)),
            pl.BlockSpec(memory_space=pl.ANY),
        ],
        out_specs=pl.BlockSpec((tm, d), lambda i: (i, 0)),
        out_shape=jax.ShapeDtypeStruct((n, d), F32),
        scratch_shapes=[
            pltpu.VMEM((2, tm, d), F32),
            pltpu.VMEM((2, tm, d), F32),
            pltpu.SemaphoreType.DMA((2,)),
        ],
        compiler_params=_cparams(("arbitrary",)),
        name="moe_combine",
    )(dest, dest, x2, wt, g2, fw, yb)


def _layer_params(l, d, w_in, conv_w, conv_b, dt_bias, a_log, d_skip, ssm_norm_w,
                  lb_all, hgrn_norm_w):
    gn = SSM_GROUPS * SSM_STATE
    n_heads = d // SSM_HEAD_DIM
    w_r, w_dt = _relayout_w_in(w_in, l)
    pad_h = lambda v: jnp.pad(v.astype(F32), (0, LANES - n_heads)).reshape(1, LANES)
    lb = jnp.maximum(lb_all[l], 0.0)
    head_of_col = jnp.arange(d, dtype=jnp.int32) // SSM_HEAD_DIM
    head_expand = (jnp.arange(LANES, dtype=jnp.int32)[:, None] == head_of_col[None, :]).astype(BF16)
    cw, cb = conv_w[l], conv_b[l]
    return dict(
        w_r=w_r, w_dt=w_dt,
        cwx=cw[:, :d], cwb=cw[:, d:d + gn], cwc=cw[:, d + gn:],
        cbx=cb[:d].reshape(1, d), cbb=cb[d:d + gn].reshape(1, gn), cbc=cb[d + gn:].reshape(1, gn),
        dt_bias=pad_h(dt_bias[l]), a_log=pad_h(a_log[l]),
        d_skip=jnp.repeat(d_skip[l].astype(F32), SSM_HEAD_DIM).reshape(1, d),
        ssm_norm_w=ssm_norm_w[l].reshape(1, d),
        head_expand=head_expand,
        log_lb=(jnp.log(lb) * LOG2E).reshape(1, d), log_1m_lb=(jnp.log1p(-lb) * LOG2E).reshape(1, d),
        hgrn_norm_w=hgrn_norm_w[l].reshape(1, d),
    )


def kernel(x, c, ada_w, ada_b, norm1_w, norm2_w, w_in, conv_w, conv_b, dt_bias, a_log, d_skip,
           ssm_norm_w, hgrn_lower_bounds, hgrn_norm_w, w_out, router_w, router_bias,
           w_gate, w_up, w_down, final_norm_w):
    bsz, seq, d = x.shape
    n_layers = ada_w.shape[0]
    n_exp = router_w.shape[1]
    n = bsz * seq
    n_assign = 2 * n
    nb = n_assign // MOE_ROWS + n_exp
    n_rows = nb * MOE_ROWS

    lb_all = jnp.cumsum(jax.nn.softmax(hgrn_lower_bounds.astype(F32), axis=0), axis=0)
    lb_all = lb_all - lb_all[0:1]
    mod = _modulation(c, ada_w, ada_b)
    rwt = router_w.astype(F32).T
    rb = router_bias.astype(F32).reshape(n_exp, 1)
    fw = final_norm_w.reshape(1, d)

    x2 = x.reshape(n, d)
    for l in range(n_layers):
        sh1, sc1, g1, sh2, sc2, g2 = [mod[l, :, k * d:(k + 1) * d].reshape(bsz, 1, d) for k in range(6)]
        p = _layer_params(l, d, w_in, conv_w, conv_b, dt_bias, a_log, d_skip, ssm_norm_w,
                          lb_all, hgrn_norm_w)
        proj, dt_raw = _in_projection(x2, norm1_w[l].reshape(1, d), sc1, sh1, p["w_r"], p["w_dt"], seq)
        ya = _ssd_branch(proj, dt_raw, p, bsz, seq, d)
        yb = _hgrn_branch(proj, p, bsz, seq, d)
        x2, h2, mi, mw, cnt = _out_projection(
            x2, ya, yb, proj, w_out[l].astype(BF16), g1, norm2_w[l].reshape(1, d), sc2, sh2, rwt, rb, seq)

        counts = cnt[:, 0].astype(jnp.int32)
        padded = (counts + MOE_ROWS - 1) // MOE_ROWS * MOE_ROWS
        pend = jnp.cumsum(padded)
        pstart = pend - padded
        is_e = mi[0:2, :, None] == jnp.arange(n_exp, dtype=jnp.int32)
        dest = jnp.sum(jnp.where(is_e, pstart, 0), axis=-1) + mi[2:4]

        xb = _dispatch(pend.astype(jnp.int32), dest, h2, n_rows)
        yexp = _experts(l, padded, xb, w_gate, w_up, w_down)
        x2 = _combine(dest, x2, mw.T, g2, fw, yexp, seq, final=(l == n_layers - 1))
    return x2.reshape(bsz, seq, d)
```

```python
import functools

import jax
import jax.numpy as jnp
from jax import lax
from jax.experimental import pallas as pl
from jax.experimental.pallas import tpu as pltpu

F32 = jnp.float32
BF16 = jnp.bfloat16
EPS = 1e-6
LOG2E = 1.4426950408889634

SSM_HEAD_DIM = 64
SSM_STATE = 128
SSM_GROUPS = 4
CONV_K = 4
HGRN_HEAD = 128
N_EXPERT_GROUPS = 4
EXPERTS_PER_GROUP = 8
LANES = 128
SUBLANES = 8

SSD_CHUNK = 128
HGRN_CHUNK = 128
HGRN_HEADS_PER_STEP = 8
HGRN_CHUNKS_PER_STEP = 2
MOE_ROWS = 256
W_CHUNKS = 4
INPROJ_TM = 1024
INPROJ_TN = 1024
OUT_TM = 256
ROUTE_ROWS = 128
TOKEN_TILE = 256
DMA_UNROLL = 32
VMEM_LIMIT = 56 * 1024 * 1024


def _cparams(sem):
    return pltpu.CompilerParams(dimension_semantics=sem, vmem_limit_bytes=VMEM_LIMIT)


def _sigmoid(x):
    return 1.0 / (1.0 + jnp.exp(-x))


def _silu(x):
    return x * _sigmoid(x)


def _split3(x):
    hi = x.astype(BF16)
    r1 = x - hi.astype(F32)
    mid = r1.astype(BF16)
    lo = (r1 - mid.astype(F32)).astype(BF16)
    return hi, mid, lo


def _dot(a, b):
    return jnp.dot(a, b, preferred_element_type=F32)


def _dot_nt(a, b):
    return lax.dot_general(a, b, (((1,), (1,)), ((), ())), preferred_element_type=F32)


def _sel_dot(sel_bf16, x):
    hi, mid, lo = _split3(x)
    return _dot(sel_bf16, hi) + (_dot(sel_bf16, mid) + _dot(sel_bf16, lo))


def _dot_sel(x, sel_bf16):
    hi, mid, lo = _split3(x)
    return _dot(hi, sel_bf16) + (_dot(mid, sel_bf16) + _dot(lo, sel_bf16))


def _tril_ones(n):
    r = lax.broadcasted_iota(jnp.int32, (n, n), 0)
    c = lax.broadcasted_iota(jnp.int32, (n, n), 1)
    return jnp.where(r >= c, 1.0, 0.0).astype(BF16)


def _mod_kernel(c_ref, w_ref, b_ref, o_ref):
    ca = _silu(c_ref[...]).astype(BF16)
    o_ref[0] = _dot(ca, w_ref[0].astype(BF16)) + b_ref[0]


def _modulation(c, ada_w, ada_b):
    n_layers, d, n6 = ada_w.shape
    bsz = c.shape[0]
    tn = 1024
    return pl.pallas_call(
        _mod_kernel,
        grid=(n_layers, n6 // tn),
        in_specs=[
            pl.BlockSpec((bsz, d), lambda l, j: (0, 0)),
            pl.BlockSpec((1, d, tn), lambda l, j: (l, 0, j)),
            pl.BlockSpec((1, 1, tn), lambda l, j: (l, 0, j)),
        ],
        out_specs=pl.BlockSpec((1, bsz, tn), lambda l, j: (l, 0, j)),
        out_shape=jax.ShapeDtypeStruct((n_layers, bsz, n6), F32),
        compiler_params=_cparams(("arbitrary", "arbitrary")),
        name="adaln_mod",
    )(c, ada_w, ada_b.reshape(n_layers, 1, n6))


_DIRECT, _LOAD_ONLY, _SHIFTED = 0, 1, 2


def _relayout_kernel(shift, src_ref, dst_ref, mode_ref, w_ref, o_ref, odt_ref, prev_scr):
    del src_ref, dst_ref
    t = pl.program_id(0)
    mode = mode_ref[t]

    @pl.when(mode == _DIRECT)
    def _():
        o_ref[...] = w_ref[0].astype(BF16)

    @pl.when(mode == _LOAD_ONLY)
    def _():
        head = w_ref[0, :, :LANES]
        lane = lax.broadcasted_iota(jnp.int32, head.shape, 1)
        odt_ref[...] = jnp.where(lane < shift, head, 0.0).astype(BF16)
        prev_scr[...] = w_ref[0]

    @pl.when(mode == _SHIFTED)
    def _():
        o_ref[...] = jnp.concatenate([prev_scr[:, shift:], w_ref[0, :, :shift]], axis=1).astype(BF16)
        prev_scr[...] = w_ref[0]


def _relayout_w_in(w_in, layer):
    d = w_in.shape[1]
    gn2 = 2 * SSM_GROUPS * SSM_STATE
    n_heads = d // SSM_HEAD_DIM
    tn = INPROJ_TN
    assert gn2 == tn and (2 * d) % tn == 0 and 0 < n_heads < LANES
    n_front, n_rest = 2 * d // tn, 6 * d // tn
    src = list(range(n_front)) + [n_front + 1 + k for k in range(n_rest + 1)] + [n_front]
    dst = list(range(n_front)) + [n_front - 1] + [n_front + k for k in range(n_rest)] + [n_front + n_rest]
    mode = [_DIRECT] * n_front + [_LOAD_ONLY] + [_SHIFTED] * n_rest + [_DIRECT]
    n_tiles = n_front + n_rest + 1
    grid_spec = pltpu.PrefetchScalarGridSpec(
        num_scalar_prefetch=3,
        grid=(len(src),),
        in_specs=[pl.BlockSpec((1, d, tn), lambda t, s, o, m: (layer, 0, s[t]))],
        out_specs=[
            pl.BlockSpec((d, tn), lambda t, s, o, m: (0, o[t])),
            pl.BlockSpec((d, LANES), lambda t, s, o, m: (0, 0)),
        ],
        scratch_shapes=[pltpu.VMEM((d, tn), w_in.dtype)],
    )
    return pl.pallas_call(
        functools.partial(_relayout_kernel, n_heads),
        grid_spec=grid_spec,
        out_shape=[
            jax.ShapeDtypeStruct((d, n_tiles * tn), BF16),
            jax.ShapeDtypeStruct((d, LANES), BF16),
        ],
        compiler_params=_cparams(("arbitrary",)),
        name="w_in_relayout",
    )(jnp.asarray(src, jnp.int32), jnp.asarray(dst, jnp.int32), jnp.asarray(mode, jnp.int32), w_in)


def _rms_mod(x, nw, sc, sh):
    ms = jnp.mean(x * x, axis=-1, keepdims=True)
    return (x * lax.rsqrt(ms + EPS) * nw) * (1.0 + sc) + sh


def _inproj_kernel(x_ref, nw_ref, sc_ref, sh_ref, w_ref, wdt_ref, o_ref, dt_ref, h_scr):
    @pl.when(pl.program_id(1) == 0)
    def _():
        h = _rms_mod(x_ref[...], nw_ref[...], sc_ref[0], sh_ref[0]).astype(BF16)
        h_scr[...] = h
        dt_ref[...] = _dot(h, wdt_ref[...])

    o_ref[...] = _dot(h_scr[...], w_ref[...])


def _in_projection(x2, nw, sc, sh, w_r, w_dt, seq):
    n, d = x2.shape
    np_ = w_r.shape[1]
    tm = min(INPROJ_TM, seq)
    tn = INPROJ_TN
    tiles_per_batch = seq // tm
    return pl.pallas_call(
        _inproj_kernel,
        grid=(n // tm, np_ // tn),
        in_specs=[
            pl.BlockSpec((tm, d), lambda i, j: (i, 0)),
            pl.BlockSpec((1, d), lambda i, j: (0, 0)),
            pl.BlockSpec((1, 1, d), lambda i, j: (i // tiles_per_batch, 0, 0)),
            pl.BlockSpec((1, 1, d), lambda i, j: (i // tiles_per_batch, 0, 0)),
            pl.BlockSpec((d, tn), lambda i, j: (0, j)),
            pl.BlockSpec((d, LANES), lambda i, j: (0, 0)),
        ],
        out_specs=[
            pl.BlockSpec((tm, tn), lambda i, j: (i, j)),
            pl.BlockSpec((tm, LANES), lambda i, j: (i, 0)),
        ],
        out_shape=[
            jax.ShapeDtypeStruct((n, np_), F32),
            jax.ShapeDtypeStruct((n, LANES), F32),
        ],
        scratch_shapes=[pltpu.VMEM((tm, d), BF16)],
        compiler_params=_cparams(("arbitrary", "arbitrary")),
        name="norm_inproj",
    )(x2, nw, sc, sh, w_r, w_dt)


def _causal_conv(ext_ref, u, w_ref, b_ref, first):
    q = u.shape[0]

    @pl.when(first)
    def _():
        ext_ref[0:SUBLANES, :] = jnp.zeros((SUBLANES, u.shape[1]), F32)

    ext_ref[SUBLANES:SUBLANES + q, :] = u
    acc = b_ref[...] + w_ref[CONV_K - 1:CONV_K, :] * u
    for k in range(CONV_K - 1):
        off = SUBLANES - (CONV_K - 1) + k
        acc = acc + w_ref[k:k + 1, :] * ext_ref[off:off + q, :]
    ext_ref[0:SUBLANES, :] = ext_ref[q:q + SUBLANES, :]
    return _silu(acc)


def _ssd_kernel(z_ref, xs_ref, bm_ref, cm_ref, dt_ref,
                cwx_ref, cwb_ref, cwc_ref, cbx_ref, cbb_ref, cbc_ref,
                dtb_ref, alog_ref, dskip_ref, nw_ref, exp_ref,
                o_ref,
                state_scr, extx_scr, extb_scr, extc_scr):
    q = z_ref.shape[0]
    d_inner = z_ref.shape[1]
    gw = d_inner // SSM_GROUPS
    heads_per_group = gw // SSM_HEAD_DIM
    first = pl.program_id(1) == 0

    @pl.when(first)
    def _():
        state_scr[...] = jnp.zeros(state_scr.shape, F32)

    xs = _causal_conv(extx_scr, xs_ref[...], cwx_ref, cbx_ref, first)
    bmat = _causal_conv(extb_scr, bm_ref[...], cwb_ref, cbb_ref, first)
    cmat = _causal_conv(extc_scr, cm_ref[...], cwc_ref, cbc_ref, first)

    x_dt = dt_ref[...] + dtb_ref[...]
    dt = jnp.maximum(x_dt, 0.0) + jnp.log(1.0 + jnp.exp(-jnp.abs(x_dt)))
    a = -jnp.exp(alog_ref[...])
    acs = _sel_dot(_tril_ones(q), dt * a)
    acs_last = acs[q - 1:q, :]
    stack = jnp.concatenate(
        [dt, jnp.exp(acs), jnp.exp(acs_last - acs),
         jnp.broadcast_to(jnp.exp(acs_last), (SUBLANES, LANES))], axis=0)
    wide = _dot_sel(stack, exp_ref[...])
    dt_e = wide[0:q]
    eacs_e = wide[q:2 * q]
    dec_e = wide[2 * q:3 * q]
    cd_e = wide[3 * q:3 * q + 1]

    xdt = xs * dt_e
    xdt_b = xdt.astype(BF16)
    xdec_b = (xdt * dec_e).astype(BF16)

    acs_t = acs.T
    ri = lax.broadcasted_iota(jnp.int32, (q, q), 0)
    ci = lax.broadcasted_iota(jnp.int32, (q, q), 1)
    causal = ri >= ci
    lane = lax.broadcasted_iota(jnp.int32, (q, LANES), 1)
    lo_half = lane < SSM_HEAD_DIM

    y_parts = []
    for g in range(SSM_GROUPS):
        bg = bmat[:, g * SSM_STATE:(g + 1) * SSM_STATE]
        cg = cmat[:, g * SSM_STATE:(g + 1) * SSM_STATE]
        bg_b = bg.astype(BF16)
        cg_b = cg.astype(BF16)
        cb = _dot_nt(cg_b, bg_b)
        st = state_scr[g]
        y_off = _dot(cg_b, st.astype(BF16)) * eacs_e[:, g * gw:(g + 1) * gw]
        diag = []
        for pair in range(heads_per_group // 2):
            ms = []
            for sub in range(2):
                h = g * heads_per_group + pair * 2 + sub
                rel = acs[:, h:h + 1] - acs_t[h:h + 1, :]
                lmat = jnp.exp(jnp.where(causal, rel, -jnp.inf))
                ms.append((cb * lmat).astype(BF16))
            lhs = jnp.concatenate(ms, axis=1)
            c0 = (g * heads_per_group + pair * 2) * SSM_HEAD_DIM
            xp = xdt_b[:, c0:c0 + LANES]
            zero = jnp.zeros_like(xp)
            rhs = jnp.concatenate(
                [jnp.where(lo_half, xp, zero), jnp.where(lo_half, zero, xp)], axis=0)
            diag.append(_dot(lhs, rhs))
        y_parts.append(jnp.concatenate(diag, axis=1) + y_off)
        new_state = st * cd_e[:, g * gw:(g + 1) * gw] + _dot(
            bg.T.astype(BF16), xdec_b[:, g * gw:(g + 1) * gw])
        state_scr[g] = new_state

    y = jnp.concatenate(y_parts, axis=1) + dskip_ref[...] * xs
    y = y * _silu(z_ref[...])
    outs = []
    for g in range(SSM_GROUPS):
        yg = y[:, g * gw:(g + 1) * gw]
        ms = jnp.mean(yg * yg, axis=-1, keepdims=True)
        outs.append(yg * lax.rsqrt(ms + EPS))
    o_ref[...] = (jnp.concatenate(outs, axis=1) * nw_ref[...]).astype(o_ref.dtype)


def _ssd_branch(proj, dt_raw, p, bsz, seq, d):
    q = min(SSD_CHUNK, seq)
    nc = seq // q
    gn = SSM_GROUPS * SSM_STATE
    b_blk = 8 * d // gn
    row = lambda b, c: b * nc + c
    vec = lambda width: pl.BlockSpec((1, width), lambda b, c: (0, 0))
    return pl.pallas_call(
        _ssd_kernel,
        grid=(bsz, nc),
        in_specs=[
            pl.BlockSpec((q, d), lambda b, c: (row(b, c), 0)),
            pl.BlockSpec((q, d), lambda b, c: (row(b, c), 1)),
            pl.BlockSpec((q, gn), lambda b, c: (row(b, c), b_blk)),
            pl.BlockSpec((q, gn), lambda b, c: (row(b, c), b_blk + 1)),
            pl.BlockSpec((q, LANES), lambda b, c: (row(b, c), 0)),
            pl.BlockSpec((CONV_K, d), lambda b, c: (0, 0)),
            pl.BlockSpec((CONV_K, gn), lambda b, c: (0, 0)),
            pl.BlockSpec((CONV_K, gn), lambda b, c: (0, 0)),
            vec(d), vec(gn), vec(gn),
            vec(LANES), vec(LANES), vec(d), vec(d),
            pl.BlockSpec((LANES, d), lambda b, c: (0, 0)),
        ],
        out_specs=pl.BlockSpec((q, d), lambda b, c: (row(b, c), 0)),
        out_shape=jax.ShapeDtypeStruct((bsz * seq, d), BF16),
        scratch_shapes=[
            pltpu.VMEM((SSM_GROUPS, SSM_STATE, d // SSM_GROUPS), F32),
            pltpu.VMEM((q + SUBLANES, d), F32),
            pltpu.VMEM((q + SUBLANES, gn), F32),
            pltpu.VMEM((q + SUBLANES, gn), F32),
        ],
        compiler_params=_cparams(("arbitrary", "arbitrary")),
        name="ssd_scan",
    )(proj, proj, proj, proj, dt_raw,
      p["cwx"], p["cwb"], p["cwc"], p["cbx"], p["cbb"], p["cbc"],
      p["dt_bias"], p["a_log"], p["d_skip"], p["ssm_norm_w"], p["head_expand"])


def _hgrn_chunk(q_raw, f_raw, v, g_raw, loglb, log1mlb, nw, s_t, tril, xor_idx, below):
    c = q_raw.shape[0]
    nv = c // SUBLANES
    f2 = f_raw * LOG2E
    logsig = jnp.minimum(f2, 0.0) - jnp.log2(1.0 + jnp.exp2(-jnp.abs(f2)))
    bb = log1mlb + logsig
    logf2 = jnp.maximum(loglb, bb) + jnp.log2(1.0 + jnp.exp2(-jnp.abs(loglb - bb)))
    kk = jnp.exp2(bb - f2)
    qf = _silu(q_raw)
    b = _sel_dot(tril, logf2)

    b3 = b.reshape(nv, SUBLANES, LANES)
    q3 = qf.reshape(nv, SUBLANES, LANES)
    k3 = kk.reshape(nv, SUBLANES, LANES)
    sub = lax.broadcasted_iota(jnp.int32, (1, SUBLANES, LANES), 1)

    def brow(i):
        return jnp.broadcast_to(b3[:, i:i + 1, :], b3.shape)

    levels = []
    up = (sub & 1) != 0
    levels.append((1, jnp.where(up, q3 * jnp.exp2(logf2.reshape(b3.shape)), k3).reshape(c, LANES)))
    up = (sub & 2) != 0
    d = (b3 - jnp.where(sub < 4, brow(1), brow(5))) * jnp.where(up, 1.0, -1.0)
    levels.append((2, (jnp.where(up, q3, k3) * jnp.exp2(d)).reshape(c, LANES)))
    up = sub >= 4
    d = (b3 - brow(3)) * jnp.where(up, 1.0, -1.0)
    levels.append((4, (jnp.where(up, q3, k3) * jnp.exp2(d)).reshape(c, LANES)))
    w = SUBLANES
    while w < c:
        parts = []
        for j in range(c // (2 * w)):
            lo, mid, hi = 2 * w * j, 2 * w * j + w, 2 * w * (j + 1)
            bm = b[mid - 1:mid, :]
            parts.append(kk[lo:mid] * jnp.exp2(bm - b[lo:mid]))
            parts.append(qf[mid:hi] * jnp.exp2(b[mid:hi] - bm))
        levels.append((w, jnp.concatenate(parts, axis=0)))
        w *= 2

    scores = None
    for w, xw in reversed(levels):
        xb = xw.astype(BF16)
        a_w = _dot_nt(xb, xb)
        scores = a_w if scores is None else jnp.where(xor_idx < 2 * w, a_w, scores)
    scores = jnp.where(below, scores, 0.0)

    o = _dot(scores.astype(BF16), v.astype(BF16))
    o = o + jnp.sum(qf * kk, axis=-1, keepdims=True) * v
    o = o + _dot_nt((qf * jnp.exp2(b)).astype(BF16), s_t.astype(BF16))
    b_last = b[c - 1:c, :]
    kst = (kk * jnp.exp2(b_last - b)).astype(BF16)
    s_new = s_t * jnp.exp2(b_last) + _dot(v.T.astype(BF16), kst)

    ms = jnp.mean(o * o, axis=-1, keepdims=True)
    y = o * lax.rsqrt(ms + EPS) * nw * _silu(g_raw)
    return y, s_new


def _hgrn_kernel(q_ref, f_ref, i_ref, g_ref, loglb_ref, log1mlb_ref, nw_ref, o_ref, state_scr):
    rows = q_ref.shape[0]
    heads = q_ref.shape[1] // HGRN_HEAD
    c = min(HGRN_CHUNK, rows)

    @pl.when(pl.program_id(2) == 0)
    def _():
        state_scr[...] = jnp.zeros(state_scr.shape, F32)

    tril = _tril_ones(c)
    ri = lax.broadcasted_iota(jnp.int32, (c, c), 0)
    ci = lax.broadcasted_iota(jnp.int32, (c, c), 1)
    xor_idx = ri ^ ci
    below = ri > ci
    for h in range(heads):
        cols = slice(h * HGRN_HEAD, (h + 1) * HGRN_HEAD)
        s_t = state_scr[h]
        for k in range(rows // c):
            rs = slice(k * c, (k + 1) * c)
            y, s_t = _hgrn_chunk(q_ref[rs, cols], f_ref[rs, cols], i_ref[rs, cols], g_ref[rs, cols],
                                 loglb_ref[:, cols], log1mlb_ref[:, cols], nw_ref[:, cols], s_t,
                                 tril, xor_idx, below)
            o_ref[rs, cols] = y.astype(o_ref.dtype)
        state_scr[h] = s_t


def _hgrn_branch(proj, p, bsz, seq, d):
    n_heads = d // HGRN_HEAD
    hb = min(HGRN_HEADS_PER_STEP, n_heads)
    wblk = hb * HGRN_HEAD
    rows = min(HGRN_CHUNK * HGRN_CHUNKS_PER_STEP, seq)
    nt = seq // rows
    per = d // wblk

    def sect(k):
        return pl.BlockSpec((rows, wblk), lambda b, hg, t: (b * nt + t, k * per + hg))

    vec = pl.BlockSpec((1, wblk), lambda b, hg, t: (0, hg))
    return pl.pallas_call(
        _hgrn_kernel,
        grid=(bsz, n_heads // hb, nt),
        in_specs=[sect(2), sect(3), sect(4), sect(5), vec, vec, vec],
        out_specs=pl.BlockSpec((rows, wblk), lambda b, hg, t: (b * nt + t, hg)),
        out_shape=jax.ShapeDtypeStruct((bsz * seq, d), BF16),
        scratch_shapes=[pltpu.VMEM((hb, HGRN_HEAD, HGRN_HEAD), F32)],
        compiler_params=_cparams(("arbitrary", "arbitrary", "arbitrary")),
        name="hgrn2_scan",
    )(proj, proj, proj, proj, p["log_lb"], p["log_1m_lb"], p["hgrn_norm_w"])


def _first_argmax(vals, idx, n):
    m = jnp.max(vals, axis=0, keepdims=True)
    i = jnp.min(jnp.where(vals == m, idx, float(n)), axis=0, keepdims=True)
    return m, i


def _outproj_kernel(x_ref, ya_ref, yb_ref, ga_ref, gb_ref, wout_ref, g1_ref,
                    nw2_ref, sc2_ref, sh2_ref, rwt_ref, rb_ref,
                    xo_ref, h2_ref, mi_ref, mw_ref, cnt_ref, cnt_scr):
    tm = x_ref.shape[0]

    @pl.when(pl.program_id(0) == 0)
    def _():
        cnt_scr[...] = jnp.zeros(cnt_scr.shape, F32)

    merged = (_sigmoid(ga_ref[...]).astype(BF16) * ya_ref[...]
              + _sigmoid(gb_ref[...]).astype(BF16) * yb_ref[...])
    xo_ref[...] = x_ref[...] + g1_ref[0] * _dot(merged, wout_ref[...])

    r_hi, r_mid, _ = _split3(rwt_ref[...])
    ts = min(ROUTE_ROWS, tm)
    total = cnt_scr[...]
    for s in range(tm // ts):
        total = _outproj_rows(slice(s * ts, (s + 1) * ts), total, (r_hi, r_mid),
                              nw2_ref, sc2_ref, sh2_ref, rb_ref, xo_ref, h2_ref, mi_ref, mw_ref)
    cnt_scr[...] = total
    cnt_ref[...] = jnp.broadcast_to(total, cnt_ref.shape)


def _outproj_rows(rs, base, r_split, nw2_ref, sc2_ref, sh2_ref, rb_ref, xo_ref, h2_ref, mi_ref, mw_ref):
    tm = rs.stop - rs.start
    n_exp = rb_ref.shape[0]
    r_hi, r_mid = r_split
    h2 = _rms_mod(xo_ref[rs, :], nw2_ref[...], sc2_ref[0], sh2_ref[0])
    h2_ref[rs, :] = h2

    h_hi, h_mid, _ = _split3(h2)
    logits = _dot_nt(r_hi, h_hi) + (_dot_nt(r_hi, h_mid) + _dot_nt(r_mid, h_hi))
    scores = _sigmoid(logits)
    sel = scores + rb_ref[...]

    sub = lax.broadcasted_iota(jnp.int32, (EXPERTS_PER_GROUP, tm), 0).astype(F32)
    neg = -jnp.inf
    best = None
    for g in range(N_EXPERT_GROUPS):
        blk = sel[g * EXPERTS_PER_GROUP:(g + 1) * EXPERTS_PER_GROUP]
        m1, i1 = _first_argmax(blk, sub, EXPERTS_PER_GROUP)
        m2 = jnp.max(jnp.where(sub == i1, neg, blk), axis=0, keepdims=True)
        gs = m1 + m2
        if best is None:
            best, gi = gs, jnp.zeros_like(gs)
            in_sel, in_sc = blk, scores[0:EXPERTS_PER_GROUP]
        else:
            upd = gs > best
            best = jnp.where(upd, gs, best)
            gi = jnp.where(upd, float(g), gi)
            in_sel = jnp.where(upd, blk, in_sel)
            in_sc = jnp.where(upd, scores[g * EXPERTS_PER_GROUP:(g + 1) * EXPERTS_PER_GROUP], in_sc)
    _, l1 = _first_argmax(in_sel, sub, EXPERTS_PER_GROUP)
    _, l2 = _first_argmax(jnp.where(sub == l1, neg, in_sel), sub, EXPERTS_PER_GROUP)
    s1 = jnp.sum(jnp.where(sub == l1, in_sc, 0.0), axis=0, keepdims=True)
    s2 = jnp.sum(jnp.where(sub == l2, in_sc, 0.0), axis=0, keepdims=True)
    e1 = gi * EXPERTS_PER_GROUP + l1
    e2 = gi * EXPERTS_PER_GROUP + l2
    mw_ref[0:1, rs] = s1 / (s1 + s2)
    mw_ref[1:2, rs] = s2 / (s1 + s2)

    eio = lax.broadcasted_iota(jnp.int32, (n_exp, tm), 0).astype(F32)
    oh1 = eio == e1
    oh2 = eio == e2
    onehot = jnp.where(oh1 | oh2, 1.0, 0.0)
    tr = lax.broadcasted_iota(jnp.int32, (tm, tm), 0)
    tc = lax.broadcasted_iota(jnp.int32, (tm, tm), 1)
    before = jnp.where(tr < tc, 1.0, 0.0).astype(BF16)
    pref = _dot(onehot.astype(BF16), before) + base
    r1 = jnp.sum(jnp.where(oh1, pref, 0.0), axis=0, keepdims=True)
    r2 = jnp.sum(jnp.where(oh2, pref, 0.0), axis=0, keepdims=True)
    mi_ref[0:1, rs] = e1.astype(jnp.int32)
    mi_ref[1:2, rs] = e2.astype(jnp.int32)
    mi_ref[2:3, rs] = r1.astype(jnp.int32)
    mi_ref[3:4, rs] = r2.astype(jnp.int32)
    return base + jnp.sum(onehot, axis=1, keepdims=True)


def _out_projection(x2, ya, yb, proj, w_out_b, g1, nw2, sc2, sh2, rwt, rb, seq):
    n, d = x2.shape
    n_exp = rwt.shape[0]
    tm = min(OUT_TM, seq)
    tpb = seq // tm
    bvec = pl.BlockSpec((1, 1, d), lambda i: (i // tpb, 0, 0))
    return pl.pallas_call(
        _outproj_kernel,
        grid=(n // tm,),
        in_specs=[
            pl.BlockSpec((tm, d), lambda i: (i, 0)),
            pl.BlockSpec((tm, d), lambda i: (i, 0)),
            pl.BlockSpec((tm, d), lambda i: (i, 0)),
            pl.BlockSpec((tm, d), lambda i: (i, 6)),
            pl.BlockSpec((tm, d), lambda i: (i, 7)),
            pl.BlockSpec((d, d), lambda i: (0, 0)),
            bvec,
            pl.BlockSpec((1, d), lambda i: (0, 0)),
            bvec, bvec,
            pl.BlockSpec((n_exp, d), lambda i: (0, 0)),
            pl.BlockSpec((n_exp, 1), lambda i: (0, 0)),
        ],
        out_specs=[
            pl.BlockSpec((tm, d), lambda i: (i, 0)),
            pl.BlockSpec((tm, d), lambda i: (i, 0)),
            pl.BlockSpec((4, tm), lambda i: (0, i)),
            pl.BlockSpec((2, tm), lambda i: (0, i)),
            pl.BlockSpec((n_exp, LANES), lambda i: (0, 0)),
        ],
        out_shape=[
            jax.ShapeDtypeStruct((n, d), F32),
            jax.ShapeDtypeStruct((n, d), F32),
            jax.ShapeDtypeStruct((4, n), jnp.int32),
            jax.ShapeDtypeStruct((2, n), F32),
            jax.ShapeDtypeStruct((n_exp, LANES), F32),
        ],
        scratch_shapes=[pltpu.VMEM((n_exp, 1), F32)],
        compiler_params=_cparams(("arbitrary",)),
        name="merge_outproj_router",
    )(x2, ya, yb, proj, proj, w_out_b, g1, nw2, sc2, sh2, rwt, rb)


def _row_copy(src_ref, src_row, dst_ref, dst_row, sem):
    return pltpu.make_async_copy(src_ref.at[pl.ds(src_row, 1)], dst_ref.at[pl.ds(dst_row, 1)], sem)


def _dispatch_kernel(pend_ref, dest_ref, h_ref, xb_ref, zero_scr, sem, zero_sem):
    tm = h_ref.shape[0]
    n_exp = pend_ref.shape[0]

    @pl.when(pl.program_id(0) == 0)
    def _():
        zero_scr[...] = jnp.zeros(zero_scr.shape, F32)

        def clear(row0):
            row0 = pl.multiple_of(row0, MOE_ROWS)
            return pltpu.make_async_copy(zero_scr, xb_ref.at[pl.ds(row0, MOE_ROWS)], zero_sem)

        def last_block(e):
            return clear(jnp.maximum(pend_ref[e] - MOE_ROWS, 0))

        for e in range(n_exp):
            last_block(e).start()
        for e in range(n_exp):
            last_block(e).wait()

        n_used = pend_ref[n_exp - 1] // MOE_ROWS
        n_blocks = xb_ref.shape[0] // MOE_ROWS

        def start_tail(j, carry):
            clear(j * MOE_ROWS).start()
            return carry

        def wait_tail(j, carry):
            clear(j * MOE_ROWS).wait()
            return carry

        lax.fori_loop(n_used, n_blocks, start_tail, 0)
        lax.fori_loop(n_used, n_blocks, wait_tail, 0)

    def start(t, carry):
        _row_copy(h_ref, t, xb_ref, dest_ref[0, t], sem).start(priority=0)
        _row_copy(h_ref, t, xb_ref, dest_ref[1, t], sem).start(priority=1)
        return carry

    lax.fori_loop(0, tm, start, 0, unroll=DMA_UNROLL)
    for _ in range(2):
        pltpu.make_async_copy(h_ref, xb_ref.at[pl.ds(0, tm)], sem).wait()


def _dispatch(pend, dest, h2, n_rows):
    n, d = h2.shape
    tm = min(TOKEN_TILE, n)
    grid_spec = pltpu.PrefetchScalarGridSpec(
        num_scalar_prefetch=1,
        grid=(n // tm,),
        in_specs=[
            pl.BlockSpec((2, tm), lambda i, pe: (0, i), memory_space=pltpu.SMEM),
            pl.BlockSpec((tm, d), lambda i, pe: (i, 0)),
        ],
        out_specs=pl.BlockSpec(memory_space=pl.ANY),
        scratch_shapes=[
            pltpu.VMEM((MOE_ROWS, d), F32),
            pltpu.SemaphoreType.DMA(()),
            pltpu.SemaphoreType.DMA(()),
        ],
    )
    return pl.pallas_call(
        _dispatch_kernel,
        grid_spec=grid_spec,
        out_shape=jax.ShapeDtypeStruct((n_rows, d), F32),
        compiler_params=_cparams(("arbitrary",)),
        name="moe_dispatch",
    )(pend, dest, h2)


def _expert_schedule(padded, n_steps):
    n_exp = padded.shape[0]
    eids = jnp.arange(n_exp, dtype=jnp.int32)
    nblk = padded // MOE_ROWS
    blk0 = (jnp.cumsum(padded) - padded) // MOE_ROWS
    nonempty = nblk > 0
    rank = jnp.cumsum(nonempty.astype(jnp.int32)) - 1
    n_nonempty = rank[-1] + 1
    has_next = nonempty & (rank + 1 < n_nonempty)
    seg_len = jnp.where(nonempty, jnp.maximum(nblk, jnp.where(has_next, W_CHUNKS, 0)), 0)
    seg_end = W_CHUNKS + jnp.cumsum(seg_len)
    seg_start = seg_end - seg_len
    total = seg_end[-1]
    by_rank = jnp.sum(jnp.where(nonempty[None, :] & (rank[None, :] == eids[:, None]), eids[None, :], 0), axis=1)

    s = jnp.arange(n_steps, dtype=jnp.int32)
    in_prologue = s < W_CHUNKS
    active = (s >= W_CHUNKS) & (s < total)
    e = jnp.minimum(jnp.sum((seg_end[None, :] <= s[:, None]).astype(jnp.int32), axis=1), n_exp - 1)
    pick = lambda v: jnp.sum(jnp.where(eids[None, :] == e[:, None], v[None, :], 0), axis=1)
    off = s - pick(seg_start)
    nblk_s, len_s = pick(nblk), pick(seg_len)
    do_compute = active & (off < nblk_s)
    last_block = (total > W_CHUNKS) * (jnp.sum(nblk) - 1)
    x_block = jnp.where(active, pick(blk0) + jnp.minimum(off, nblk_s - 1),
                        jnp.where(in_prologue, 0, last_block))
    do_convert = in_prologue | (active & (pick(has_next.astype(jnp.int32)) > 0) & (off >= len_s - W_CHUNKS))
    conv_no = jnp.maximum(jnp.cumsum(do_convert.astype(jnp.int32)) - 1, 0)
    conv_rank = conv_no // W_CHUNKS
    conv_expert = jnp.sum(jnp.where(eids[None, :] == conv_rank[:, None], by_rank[None, :], 0), axis=1)
    as_i32 = lambda v: v.astype(jnp.int32)
    return (as_i32(x_block), as_i32(do_compute), as_i32(pick(rank) % 2), as_i32(do_convert),
            as_i32(conv_expert), as_i32(conv_no % W_CHUNKS), as_i32(conv_rank % 2))


def _expert_kernel(xblk_ref, compute_ref, cslot_ref, convert_ref, cexp_ref, chunk_ref, vslot_ref,
                   x_ref, wgc_ref, wuc_ref, wdc_ref, o_ref, wg_scr, wu_scr, wd_scr):
    del xblk_ref, cexp_ref
    s = pl.program_id(0)
    ch = wgc_ref.shape[3]

    @pl.when(convert_ref[s] != 0)
    def _():
        slot = vslot_ref[s]
        for c in range(W_CHUNKS):
            @pl.when(chunk_ref[s] == c)
            def _():
                wg_scr[slot, :, c * ch:(c + 1) * ch] = wgc_ref[0, 0].astype(BF16)
                wu_scr[slot, :, c * ch:(c + 1) * ch] = wuc_ref[0, 0].astype(BF16)
                wd_scr[slot, c * ch:(c + 1) * ch, :] = wdc_ref[0, 0].astype(BF16)

    @pl.when(compute_ref[s] != 0)
    def _():
        slot = cslot_ref[s]
        x = x_ref[...].astype(BF16)
        hid = _silu(_dot(x, wg_scr[slot])) * _dot(x, wu_scr[slot])
        o_ref[...] = _dot(hid.astype(BF16), wd_scr[slot])


def _experts(layer, padded, xb, w_gate, w_up, w_down):
    n_rows, d = xb.shape
    n_exp, f = w_gate.shape[1], w_gate.shape[3]
    ch = f // W_CHUNKS
    n_steps = n_rows // MOE_ROWS + W_CHUNKS * (n_exp + 1)
    tables = _expert_schedule(padded, n_steps)
    n_tab = len(tables)
    grid_spec = pltpu.PrefetchScalarGridSpec(
        num_scalar_prefetch=n_tab,
        grid=(n_steps,),
        in_specs=[
            pl.BlockSpec((MOE_ROWS, d), lambda s, xb_, cp, cs, cv, ce, ck, vs: (xb_[s], 0)),
            pl.BlockSpec((1, 1, d, ch), lambda s, xb_, cp, cs, cv, ce, ck, vs: (layer, ce[s], 0, ck[s])),
            pl.BlockSpec((1, 1, d, ch), lambda s, xb_, cp, cs, cv, ce, ck, vs: (layer, ce[s], 0, ck[s])),
            pl.BlockSpec((1, 1, ch, d), lambda s, xb_, cp, cs, cv, ce, ck, vs: (layer, ce[s], ck[s], 0)),
        ],
        out_specs=pl.BlockSpec((MOE_ROWS, d), lambda s, xb_, cp, cs, cv, ce, ck, vs: (xb_[s], 0)),
        scratch_shapes=[
            pltpu.VMEM((2, d, f), BF16),
            pltpu.VMEM((2, d, f), BF16),
            pltpu.VMEM((2, f, d), BF16),
        ],
    )
    return pl.pallas_call(
        _expert_kernel,
        grid_spec=grid_spec,
        out_shape=jax.ShapeDtypeStruct((n_rows, d), F32),
        input_output_aliases={n_tab: 0},
        compiler_params=_cparams(("arbitrary",)),
        name="moe_experts",
    )(*tables, xb, w_gate, w_up, w_down)


def _combine_kernel(final, dest_ref, dest_next_ref, x_ref, wt_ref, g2_ref, fw_ref, yb_ref, o_ref,
                    r0_scr, r1_scr, sems):
    tm = x_ref.shape[0]
    i = pl.program_id(0)
    slot = i % 2

    def gather(idx_ref, s):
        def start(t, carry):
            _row_copy(yb_ref, idx_ref[0, t], r0_scr.at[s], t, sems.at[s]).start(priority=0)
            _row_copy(yb_ref, idx_ref[1, t], r1_scr.at[s], t, sems.at[s]).start(priority=1)
            return carry

        lax.fori_loop(0, tm, start, 0, unroll=DMA_UNROLL)

    @pl.when(i == 0)
    def _():
        gather(dest_ref, slot)

    @pl.when(i + 1 < pl.num_programs(0))
    def _():
        gather(dest_next_ref, 1 - slot)

    pltpu.make_async_copy(yb_ref.at[pl.ds(0, tm)], r0_scr.at[slot], sems.at[slot]).wait()
    pltpu.make_async_copy(yb_ref.at[pl.ds(0, tm)], r1_scr.at[slot], sems.at[slot]).wait()

    wt = wt_ref[...]
    ffn = wt[:, 0:1] * r0_scr[slot] + wt[:, 1:2] * r1_scr[slot]
    xn = x_ref[...] + g2_ref[0] * ffn
    if final:
        ms = jnp.mean(xn * xn, axis=-1, keepdims=True)
        xn = xn * lax.rsqrt(ms + EPS) * fw_ref[...]
    o_ref[...] = xn


def _combine(dest, x2, wt, g2, fw, yb, seq, final):
    n, d = x2.shape
    tm = min(TOKEN_TILE, seq)
    tpb = seq // tm
    last = n // tm - 1
    return pl.pallas_call(
        functools.partial(_combine_kernel, final),
        grid=(n // tm,),
        in_specs=[
            pl.BlockSpec((2, tm), lambda i: (0, i), memory_space=pltpu.SMEM),
            pl.BlockSpec((2, tm), lambda i: (0, jnp.minimum(i + 1, last)), memory_space=pltpu.SMEM),
            pl.BlockSpec((tm, d), lambda i: (i, 0)),
            pl.BlockSpec((tm, 2), lambda i: (i, 0)),
            pl.BlockSpec((1, 1, d), lambda i: (i // tpb, 0, 0)),
            pl.BlockSpec((1, d), lambda i: (0, 0)),
            pl.BlockSpec(memory_space=pl.ANY),
        ],
        out_specs=pl.BlockSpec((tm, d), lambda i: (i, 0)),
        out_shape=jax.ShapeDtypeStruct((n, d), F32),
        scratch_shapes=[
            pltpu.VMEM((2, tm, d), F32),
            pltpu.VMEM((2, tm, d), F32),
            pltpu.SemaphoreType.DMA((2,)),
        ],
        compiler_params=_cparams(("arbitrary",)),
        name="moe_combine",
    )(dest, dest, x2, wt, g2, fw, yb)


def _layer_params(l, d, w_in, conv_w, conv_b, dt_bias, a_log, d_skip, ssm_norm_w,
                  lb_all, hgrn_norm_w):
    gn = SSM_GROUPS * SSM_STATE
    n_heads = d // SSM_HEAD_DIM
    w_r, w_dt = _relayout_w_in(w_in, l)
    pad_h = lambda v: jnp.pad(v.astype(F32), (0, LANES - n_heads)).reshape(1, LANES)
    lb = jnp.maximum(lb_all[l], 0.0)
    head_of_col = jnp.arange(d, dtype=jnp.int32) // SSM_HEAD_DIM
    head_expand = (jnp.arange(LANES, dtype=jnp.int32)[:, None] == head_of_col[None, :]).astype(BF16)
    cw, cb = conv_w[l], conv_b[l]
    return dict(
        w_r=w_r, w_dt=w_dt,
        cwx=cw[:, :d], cwb=cw[:, d:d + gn], cwc=cw[:, d + gn:],
        cbx=cb[:d].reshape(1, d), cbb=cb[d:d + gn].reshape(1, gn), cbc=cb[d + gn:].reshape(1, gn),
        dt_bias=pad_h(dt_bias[l]), a_log=pad_h(a_log[l]),
        d_skip=jnp.repeat(d_skip[l].astype(F32), SSM_HEAD_DIM).reshape(1, d),
        ssm_norm_w=ssm_norm_w[l].reshape(1, d),
        head_expand=head_expand,
        log_lb=(jnp.log(lb) * LOG2E).reshape(1, d), log_1m_lb=(jnp.log1p(-lb) * LOG2E).reshape(1, d),
        hgrn_norm_w=hgrn_norm_w[l].reshape(1, d),
    )


def kernel(x, c, ada_w, ada_b, norm1_w, norm2_w, w_in, conv_w, conv_b, dt_bias, a_log, d_skip,
           ssm_norm_w, hgrn_lower_bounds, hgrn_norm_w, w_out, router_w, router_bias,
           w_gate, w_up, w_down, final_norm_w):
    bsz, seq, d = x.shape
    n_layers = ada_w.shape[0]
    n_exp = router_w.shape[1]
    n = bsz * seq
    n_assign = 2 * n
    nb = n_assign // MOE_ROWS + n_exp
    n_rows = nb * MOE_ROWS

    lb_all = jnp.cumsum(jax.nn.softmax(hgrn_lower_bounds.astype(F32), axis=0), axis=0)
    lb_all = lb_all - lb_all[0:1]
    mod = _modulation(c, ada_w, ada_b)
    rwt = router_w.astype(F32).T
    rb = router_bias.astype(F32).reshape(n_exp, 1)
    fw = final_norm_w.reshape(1, d)
    w_in = w_in.astype(BF16)

    x2 = x.reshape(n, d)
    for l in range(n_layers):
        sh1, sc1, g1, sh2, sc2, g2 = [mod[l, :, k * d:(k + 1) * d].reshape(bsz, 1, d) for k in range(6)]
        p = _layer_params(l, d, w_in, conv_w, conv_b, dt_bias, a_log, d_skip, ssm_norm_w,
                          lb_all, hgrn_norm_w)
        proj, dt_raw = _in_projection(x2, norm1_w[l].reshape(1, d), sc1, sh1, p["w_r"], p["w_dt"], seq)
        ya = _ssd_branch(proj, dt_raw, p, bsz, seq, d)
        yb = _hgrn_branch(proj, p, bsz, seq, d)
        x2, h2, mi, mw, cnt = _out_projection(
            x2, ya, yb, proj, w_out[l].astype(BF16), g1, norm2_w[l].reshape(1, d), sc2, sh2, rwt, rb, seq)

        counts = cnt[:, 0].astype(jnp.int32)
        padded = (counts + MOE_ROWS - 1) // MOE_ROWS * MOE_ROWS
        pend = jnp.cumsum(padded)
        pstart = pend - padded
        is_e = mi[0:2, :, None] == jnp.arange(n_exp, dtype=jnp.int32)
        dest = jnp.sum(jnp.where(is_e, pstart, 0), axis=-1) + mi[2:4]

        xb = _dispatch(pend.astype(jnp.int32), dest, h2, n_rows)
        yexp = _experts(l, padded, xb, w_gate, w_up, w_down)
        x2 = _combine(dest, x2, mw.T, g2, fw, yexp, seq, final=(l == n_layers - 1))
    return x2.reshape(bsz, seq, d)
```

```python
import functools

import jax
import jax.numpy as jnp
from jax import lax
from jax.experimental import pallas as pl
from jax.experimental.pallas import tpu as pltpu

F32 = jnp.float32
BF16 = jnp.bfloat16
EPS = 1e-6
LOG2E = 1.4426950408889634

SSM_HEAD_DIM = 64
SSM_STATE = 128
SSM_GROUPS = 4
CONV_K = 4
HGRN_HEAD = 128
N_EXPERT_GROUPS = 4
EXPERTS_PER_GROUP = 8
LANES = 128
SUBLANES = 8

SSD_CHUNK = 128
HGRN_CHUNK = 128
HGRN_HEADS_PER_STEP = 8
HGRN_CHUNKS_PER_STEP = 4
MOE_ROWS = 256
W_CHUNKS = 4
INPROJ_TM = 1024
INPROJ_TN = 1024
OUT_TM = 256
ROUTE_ROWS = 128
TOKEN_TILE = 256
DMA_UNROLL = 32
VMEM_LIMIT = 56 * 1024 * 1024


def _cparams(sem):
    return pltpu.CompilerParams(dimension_semantics=sem, vmem_limit_bytes=VMEM_LIMIT)


def _sigmoid(x):
    return 1.0 / (1.0 + jnp.exp(-x))


def _silu(x):
    return x * _sigmoid(x)


def _split3(x):
    hi = x.astype(BF16)
    r1 = x - hi.astype(F32)
    mid = r1.astype(BF16)
    lo = (r1 - mid.astype(F32)).astype(BF16)
    return hi, mid, lo


def _dot(a, b):
    return jnp.dot(a, b, preferred_element_type=F32)


def _dot_nt(a, b):
    return lax.dot_general(a, b, (((1,), (1,)), ((), ())), preferred_element_type=F32)


def _sel_dot(sel_bf16, x):
    hi, mid, lo = _split3(x)
    return _dot(sel_bf16, hi) + (_dot(sel_bf16, mid) + _dot(sel_bf16, lo))


def _dot_sel(x, sel_bf16):
    hi, mid, lo = _split3(x)
    return _dot(hi, sel_bf16) + (_dot(mid, sel_bf16) + _dot(lo, sel_bf16))


def _tril_ones(n):
    r = lax.broadcasted_iota(jnp.int32, (n, n), 0)
    c = lax.broadcasted_iota(jnp.int32, (n, n), 1)
    return jnp.where(r >= c, 1.0, 0.0).astype(BF16)


def _mod_kernel(c_ref, w_ref, b_ref, o_ref):
    ca = _silu(c_ref[...]).astype(BF16)
    o_ref[0] = _dot(ca, w_ref[0].astype(BF16)) + b_ref[0]


def _modulation(c, ada_w, ada_b):
    n_layers, d, n6 = ada_w.shape
    bsz = c.shape[0]
    tn = 1024
    return pl.pallas_call(
        _mod_kernel,
        grid=(n_layers, n6 // tn),
        in_specs=[
            pl.BlockSpec((bsz, d), lambda l, j: (0, 0)),
            pl.BlockSpec((1, d, tn), lambda l, j: (l, 0, j)),
            pl.BlockSpec((1, 1, tn), lambda l, j: (l, 0, j)),
        ],
        out_specs=pl.BlockSpec((1, bsz, tn), lambda l, j: (l, 0, j)),
        out_shape=jax.ShapeDtypeStruct((n_layers, bsz, n6), F32),
        compiler_params=_cparams(("arbitrary", "arbitrary")),
        name="adaln_mod",
    )(c, ada_w, ada_b.reshape(n_layers, 1, n6))


_DIRECT, _LOAD_ONLY, _SHIFTED = 0, 1, 2


def _relayout_kernel(shift, src_ref, dst_ref, mode_ref, w_ref, o_ref, odt_ref, prev_scr):
    del src_ref, dst_ref
    t = pl.program_id(0)
    mode = mode_ref[t]

    @pl.when(mode == _DIRECT)
    def _():
        o_ref[...] = w_ref[0].astype(BF16)

    @pl.when(mode == _LOAD_ONLY)
    def _():
        head = w_ref[0, :, :LANES]
        lane = lax.broadcasted_iota(jnp.int32, head.shape, 1)
        odt_ref[...] = jnp.where(lane < shift, head, 0.0).astype(BF16)
        prev_scr[...] = w_ref[0]

    @pl.when(mode == _SHIFTED)
    def _():
        o_ref[...] = jnp.concatenate([prev_scr[:, shift:], w_ref[0, :, :shift]], axis=1).astype(BF16)
        prev_scr[...] = w_ref[0]


def _relayout_w_in(w_in, layer):
    d = w_in.shape[1]
    gn2 = 2 * SSM_GROUPS * SSM_STATE
    n_heads = d // SSM_HEAD_DIM
    tn = INPROJ_TN
    assert gn2 == tn and (2 * d) % tn == 0 and 0 < n_heads < LANES
    n_front, n_rest = 2 * d // tn, 6 * d // tn
    src = list(range(n_front)) + [n_front + 1 + k for k in range(n_rest + 1)] + [n_front]
    dst = list(range(n_front)) + [n_front - 1] + [n_front + k for k in range(n_rest)] + [n_front + n_rest]
    mode = [_DIRECT] * n_front + [_LOAD_ONLY] + [_SHIFTED] * n_rest + [_DIRECT]
    n_tiles = n_front + n_rest + 1
    grid_spec = pltpu.PrefetchScalarGridSpec(
        num_scalar_prefetch=3,
        grid=(len(src),),
        in_specs=[pl.BlockSpec((1, d, tn), lambda t, s, o, m: (layer, 0, s[t]))],
        out_specs=[
            pl.BlockSpec((d, tn), lambda t, s, o, m: (0, o[t])),
            pl.BlockSpec((d, LANES), lambda t, s, o, m: (0, 0)),
        ],
        scratch_shapes=[pltpu.VMEM((d, tn), w_in.dtype)],
    )
    return pl.pallas_call(
        functools.partial(_relayout_kernel, n_heads),
        grid_spec=grid_spec,
        out_shape=[
            jax.ShapeDtypeStruct((d, n_tiles * tn), BF16),
            jax.ShapeDtypeStruct((d, LANES), BF16),
        ],
        compiler_params=_cparams(("arbitrary",)),
        name="w_in_relayout",
    )(jnp.asarray(src, jnp.int32), jnp.asarray(dst, jnp.int32), jnp.asarray(mode, jnp.int32), w_in)


def _rms_mod(x, nw, sc, sh):
    ms = jnp.mean(x * x, axis=-1, keepdims=True)
    return (x * lax.rsqrt(ms + EPS) * nw) * (1.0 + sc) + sh


def _inproj_kernel(x_ref, nw_ref, sc_ref, sh_ref, w_ref, wdt_ref, o_ref, dt_ref, h_scr):
    @pl.when(pl.program_id(1) == 0)
    def _():
        h = _rms_mod(x_ref[...], nw_ref[...], sc_ref[0], sh_ref[0]).astype(BF16)
        h_scr[...] = h
        dt_ref[...] = _dot(h, wdt_ref[...])

    o_ref[...] = _dot(h_scr[...], w_ref[...])


def _in_projection(x2, nw, sc, sh, w_r, w_dt, seq):
    n, d = x2.shape
    np_ = w_r.shape[1]
    tm = min(INPROJ_TM, seq)
    tn = INPROJ_TN
    tiles_per_batch = seq // tm
    return pl.pallas_call(
        _inproj_kernel,
        grid=(n // tm, np_ // tn),
        in_specs=[
            pl.BlockSpec((tm, d), lambda i, j: (i, 0)),
            pl.BlockSpec((1, d), lambda i, j: (0, 0)),
            pl.BlockSpec((1, 1, d), lambda i, j: (i // tiles_per_batch, 0, 0)),
            pl.BlockSpec((1, 1, d), lambda i, j: (i // tiles_per_batch, 0, 0)),
            pl.BlockSpec((d, tn), lambda i, j: (0, j)),
            pl.BlockSpec((d, LANES), lambda i, j: (0, 0)),
        ],
        out_specs=[
            pl.BlockSpec((tm, tn), lambda i, j: (i, j)),
            pl.BlockSpec((tm, LANES), lambda i, j: (i, 0)),
        ],
        out_shape=[
            jax.ShapeDtypeStruct((n, np_), F32),
            jax.ShapeDtypeStruct((n, LANES), F32),
        ],
        scratch_shapes=[pltpu.VMEM((tm, d), BF16)],
        compiler_params=_cparams(("arbitrary", "arbitrary")),
        name="norm_inproj",
    )(x2, nw, sc, sh, w_r, w_dt)


def _causal_conv(ext_ref, u, w_ref, b_ref, first):
    q = u.shape[0]

    @pl.when(first)
    def _():
        ext_ref[0:SUBLANES, :] = jnp.zeros((SUBLANES, u.shape[1]), F32)

    ext_ref[SUBLANES:SUBLANES + q, :] = u
    acc = b_ref[...] + w_ref[CONV_K - 1:CONV_K, :] * u
    for k in range(CONV_K - 1):
        off = SUBLANES - (CONV_K - 1) + k
        acc = acc + w_ref[k:k + 1, :] * ext_ref[off:off + q, :]
    ext_ref[0:SUBLANES, :] = ext_ref[q:q + SUBLANES, :]
    return _silu(acc)


def _ssd_kernel(z_ref, xs_ref, bm_ref, cm_ref, dt_ref,
                cwx_ref, cwb_ref, cwc_ref, cbx_ref, cbb_ref, cbc_ref,
                dtb_ref, alog_ref, dskip_ref, nw_ref, exp_ref,
                o_ref,
                state_scr, extx_scr, extb_scr, extc_scr):
    q = z_ref.shape[0]
    d_inner = z_ref.shape[1]
    gw = d_inner // SSM_GROUPS
    heads_per_group = gw // SSM_HEAD_DIM
    first = pl.program_id(1) == 0

    @pl.when(first)
    def _():
        state_scr[...] = jnp.zeros(state_scr.shape, F32)

    xs = _causal_conv(extx_scr, xs_ref[...], cwx_ref, cbx_ref, first)
    bmat = _causal_conv(extb_scr, bm_ref[...], cwb_ref, cbb_ref, first)
    cmat = _causal_conv(extc_scr, cm_ref[...], cwc_ref, cbc_ref, first)

    x_dt = dt_ref[...] + dtb_ref[...]
    dt = jnp.maximum(x_dt, 0.0) + jnp.log(1.0 + jnp.exp(-jnp.abs(x_dt)))
    a = -jnp.exp(alog_ref[...])
    acs = _sel_dot(_tril_ones(q), dt * a)
    acs_last = acs[q - 1:q, :]
    stack = jnp.concatenate(
        [dt, jnp.exp(acs), jnp.exp(acs_last - acs),
         jnp.broadcast_to(jnp.exp(acs_last), (SUBLANES, LANES))], axis=0)
    wide = _dot_sel(stack, exp_ref[...])
    dt_e = wide[0:q]
    eacs_e = wide[q:2 * q]
    dec_e = wide[2 * q:3 * q]
    cd_e = wide[3 * q:3 * q + 1]

    xdt = xs * dt_e
    xdt_b = xdt.astype(BF16)
    xdec_b = (xdt * dec_e).astype(BF16)

    acs_t = acs.T
    ri = lax.broadcasted_iota(jnp.int32, (q, q), 0)
    ci = lax.broadcasted_iota(jnp.int32, (q, q), 1)
    causal = ri >= ci
    lane = lax.broadcasted_iota(jnp.int32, (q, LANES), 1)
    lo_half = lane < SSM_HEAD_DIM

    y_parts = []
    for g in range(SSM_GROUPS):
        bg = bmat[:, g * SSM_STATE:(g + 1) * SSM_STATE]
        cg = cmat[:, g * SSM_STATE:(g + 1) * SSM_STATE]
        bg_b = bg.astype(BF16)
        cg_b = cg.astype(BF16)
        cb = _dot_nt(cg_b, bg_b)
        st = state_scr[g]
        y_off = _dot(cg_b, st.astype(BF16)) * eacs_e[:, g * gw:(g + 1) * gw]
        diag = []
        for pair in range(heads_per_group // 2):
            ms = []
            for sub in range(2):
                h = g * heads_per_group + pair * 2 + sub
                rel = acs[:, h:h + 1] - acs_t[h:h + 1, :]
                lmat = jnp.exp(jnp.where(causal, rel, -jnp.inf))
                ms.append((cb * lmat).astype(BF16))
            lhs = jnp.concatenate(ms, axis=1)
            c0 = (g * heads_per_group + pair * 2) * SSM_HEAD_DIM
            xp = xdt_b[:, c0:c0 + LANES]
            zero = jnp.zeros_like(xp)
            rhs = jnp.concatenate(
                [jnp.where(lo_half, xp, zero), jnp.where(lo_half, zero, xp)], axis=0)
            diag.append(_dot(lhs, rhs))
        y_parts.append(jnp.concatenate(diag, axis=1) + y_off)
        new_state = st * cd_e[:, g * gw:(g + 1) * gw] + _dot(
            bg.T.astype(BF16), xdec_b[:, g * gw:(g + 1) * gw])
        state_scr[g] = new_state

    y = jnp.concatenate(y_parts, axis=1) + dskip_ref[...] * xs
    y = y * _silu(z_ref[...])
    outs = []
    for g in range(SSM_GROUPS):
        yg = y[:, g * gw:(g + 1) * gw]
        ms = jnp.mean(yg * yg, axis=-1, keepdims=True)
        outs.append(yg * lax.rsqrt(ms + EPS))
    o_ref[...] = (jnp.concatenate(outs, axis=1) * nw_ref[...]).astype(o_ref.dtype)


def _ssd_branch(proj, dt_raw, p, bsz, seq, d):
    q = min(SSD_CHUNK, seq)
    nc = seq // q
    gn = SSM_GROUPS * SSM_STATE
    b_blk = 8 * d // gn
    row = lambda b, c: b * nc + c
    vec = lambda width: pl.BlockSpec((1, width), lambda b, c: (0, 0))
    return pl.pallas_call(
        _ssd_kernel,
        grid=(bsz, nc),
        in_specs=[
            pl.BlockSpec((q, d), lambda b, c: (row(b, c), 0)),
            pl.BlockSpec((q, d), lambda b, c: (row(b, c), 1)),
            pl.BlockSpec((q, gn), lambda b, c: (row(b, c), b_blk)),
            pl.BlockSpec((q, gn), lambda b, c: (row(b, c), b_blk + 1)),
            pl.BlockSpec((q, LANES), lambda b, c: (row(b, c), 0)),
            pl.BlockSpec((CONV_K, d), lambda b, c: (0, 0)),
            pl.BlockSpec((CONV_K, gn), lambda b, c: (0, 0)),
            pl.BlockSpec((CONV_K, gn), lambda b, c: (0, 0)),
            vec(d), vec(gn), vec(gn),
            vec(LANES), vec(LANES), vec(d), vec(d),
            pl.BlockSpec((LANES, d), lambda b, c: (0, 0)),
        ],
        out_specs=pl.BlockSpec((q, d), lambda b, c: (row(b, c), 0)),
        out_shape=jax.ShapeDtypeStruct((bsz * seq, d), BF16),
        scratch_shapes=[
            pltpu.VMEM((SSM_GROUPS, SSM_STATE, d // SSM_GROUPS), F32),
            pltpu.VMEM((q + SUBLANES, d), F32),
            pltpu.VMEM((q + SUBLANES, gn), F32),
            pltpu.VMEM((q + SUBLANES, gn), F32),
        ],
        compiler_params=_cparams(("arbitrary", "arbitrary")),
        name="ssd_scan",
    )(proj, proj, proj, proj, dt_raw,
      p["cwx"], p["cwb"], p["cwc"], p["cbx"], p["cbb"], p["cbc"],
      p["dt_bias"], p["a_log"], p["d_skip"], p["ssm_norm_w"], p["head_expand"])


def _hgrn_chunk(q_raw, f_raw, v, g_raw, loglb, log1mlb, nw, s_t, tril, xor_idx, below):
    c = q_raw.shape[0]
    nv = c // SUBLANES
    f2 = f_raw * LOG2E
    logsig = jnp.minimum(f2, 0.0) - jnp.log2(1.0 + jnp.exp2(-jnp.abs(f2)))
    bb = log1mlb + logsig
    logf2 = jnp.maximum(loglb, bb) + jnp.log2(1.0 + jnp.exp2(-jnp.abs(loglb - bb)))
    kk = jnp.exp2(bb - f2)
    qf = _silu(q_raw)
    b = _sel_dot(tril, logf2)

    b3 = b.reshape(nv, SUBLANES, LANES)
    q3 = qf.reshape(nv, SUBLANES, LANES)
    k3 = kk.reshape(nv, SUBLANES, LANES)
    sub = lax.broadcasted_iota(jnp.int32, (1, SUBLANES, LANES), 1)

    def brow(i):
        return jnp.broadcast_to(b3[:, i:i + 1, :], b3.shape)

    levels = []
    up = (sub & 1) != 0
    levels.append((1, jnp.where(up, q3 * jnp.exp2(logf2.reshape(b3.shape)), k3).reshape(c, LANES)))
    up = (sub & 2) != 0
    d = (b3 - jnp.where(sub < 4, brow(1), brow(5))) * jnp.where(up, 1.0, -1.0)
    levels.append((2, (jnp.where(up, q3, k3) * jnp.exp2(d)).reshape(c, LANES)))
    up = sub >= 4
    d = (b3 - brow(3)) * jnp.where(up, 1.0, -1.0)
    levels.append((4, (jnp.where(up, q3, k3) * jnp.exp2(d)).reshape(c, LANES)))
    w = SUBLANES
    while w < c:
        parts = []
        for j in range(c // (2 * w)):
            lo, mid, hi = 2 * w * j, 2 * w * j + w, 2 * w * (j + 1)
            bm = b[mid - 1:mid, :]
            parts.append(kk[lo:mid] * jnp.exp2(bm - b[lo:mid]))
            parts.append(qf[mid:hi] * jnp.exp2(b[mid:hi] - bm))
        levels.append((w, jnp.concatenate(parts, axis=0)))
        w *= 2

    scores = None
    for w, xw in reversed(levels):
        xb = xw.astype(BF16)
        a_w = _dot_nt(xb, xb)
        scores = a_w if scores is None else jnp.where(xor_idx < 2 * w, a_w, scores)
    scores = jnp.where(below, scores, 0.0)

    o = _dot(scores.astype(BF16), v.astype(BF16))
    o = o + jnp.sum(qf * kk, axis=-1, keepdims=True) * v
    o = o + _dot_nt((qf * jnp.exp2(b)).astype(BF16), s_t.astype(BF16))
    b_last = b[c - 1:c, :]
    kst = (kk * jnp.exp2(b_last - b)).astype(BF16)
    s_new = s_t * jnp.exp2(b_last) + _dot(v.T.astype(BF16), kst)

    ms = jnp.mean(o * o, axis=-1, keepdims=True)
    y = o * lax.rsqrt(ms + EPS) * nw * _silu(g_raw)
    return y, s_new


def _hgrn_kernel(q_ref, f_ref, i_ref, g_ref, loglb_ref, log1mlb_ref, nw_ref, o_ref, state_scr):
    rows = q_ref.shape[0]
    heads = q_ref.shape[1] // HGRN_HEAD
    c = min(HGRN_CHUNK, rows)

    @pl.when(pl.program_id(2) == 0)
    def _():
        state_scr[...] = jnp.zeros(state_scr.shape, F32)

    tril = _tril_ones(c)
    ri = lax.broadcasted_iota(jnp.int32, (c, c), 0)
    ci = lax.broadcasted_iota(jnp.int32, (c, c), 1)
    xor_idx = ri ^ ci
    below = ri > ci
    for h in range(heads):
        cols = slice(h * HGRN_HEAD, (h + 1) * HGRN_HEAD)
        s_t = state_scr[h]
        for k in range(rows // c):
            rs = slice(k * c, (k + 1) * c)
            y, s_t = _hgrn_chunk(q_ref[rs, cols], f_ref[rs, cols], i_ref[rs, cols], g_ref[rs, cols],
                                 loglb_ref[:, cols], log1mlb_ref[:, cols], nw_ref[:, cols], s_t,
                                 tril, xor_idx, below)
            o_ref[rs, cols] = y.astype(o_ref.dtype)
        state_scr[h] = s_t


def _hgrn_branch(proj, p, bsz, seq, d):
    n_heads = d // HGRN_HEAD
    hb = min(HGRN_HEADS_PER_STEP, n_heads)
    wblk = hb * HGRN_HEAD
    rows = min(HGRN_CHUNK * HGRN_CHUNKS_PER_STEP, seq)
    nt = seq // rows
    per = d // wblk

    def sect(k):
        return pl.BlockSpec((rows, wblk), lambda b, hg, t: (b * nt + t, k * per + hg))

    vec = pl.BlockSpec((1, wblk), lambda b, hg, t: (0, hg))
    return pl.pallas_call(
        _hgrn_kernel,
        grid=(bsz, n_heads // hb, nt),
        in_specs=[sect(2), sect(3), sect(4), sect(5), vec, vec, vec],
        out_specs=pl.BlockSpec((rows, wblk), lambda b, hg, t: (b * nt + t, hg)),
        out_shape=jax.ShapeDtypeStruct((bsz * seq, d), BF16),
        scratch_shapes=[pltpu.VMEM((hb, HGRN_HEAD, HGRN_HEAD), F32)],
        compiler_params=_cparams(("arbitrary", "arbitrary", "arbitrary")),
        name="hgrn2_scan",
    )(proj, proj, proj, proj, p["log_lb"], p["log_1m_lb"], p["hgrn_norm_w"])


def _first_argmax(vals, idx, n):
    m = jnp.max(vals, axis=0, keepdims=True)
    i = jnp.min(jnp.where(vals == m, idx, float(n)), axis=0, keepdims=True)
    return m, i


def _outproj_kernel(x_ref, ya_ref, yb_ref, ga_ref, gb_ref, wout_ref, g1_ref,
                    nw2_ref, sc2_ref, sh2_ref, rwt_ref, rb_ref,
                    xo_ref, h2_ref, mi_ref, mw_ref, cnt_ref, cnt_scr):
    tm = x_ref.shape[0]

    @pl.when(pl.program_id(0) == 0)
    def _():
        cnt_scr[...] = jnp.zeros(cnt_scr.shape, F32)

    merged = (_sigmoid(ga_ref[...]).astype(BF16) * ya_ref[...]
              + _sigmoid(gb_ref[...]).astype(BF16) * yb_ref[...])
    xo_ref[...] = x_ref[...] + g1_ref[0] * _dot(merged, wout_ref[...])

    r_hi, r_mid, _ = _split3(rwt_ref[...])
    ts = min(ROUTE_ROWS, tm)
    total = cnt_scr[...]
    for s in range(tm // ts):
        total = _outproj_rows(slice(s * ts, (s + 1) * ts), total, (r_hi, r_mid),
                              nw2_ref, sc2_ref, sh2_ref, rb_ref, xo_ref, h2_ref, mi_ref, mw_ref)
    cnt_scr[...] = total
    cnt_ref[...] = jnp.broadcast_to(total, cnt_ref.shape)


def _outproj_rows(rs, base, r_split, nw2_ref, sc2_ref, sh2_ref, rb_ref, xo_ref, h2_ref, mi_ref, mw_ref):
    tm = rs.stop - rs.start
    n_exp = rb_ref.shape[0]
    r_hi, r_mid = r_split
    h2 = _rms_mod(xo_ref[rs, :], nw2_ref[...], sc2_ref[0], sh2_ref[0])
    h2_ref[rs, :] = h2

    h_hi, h_mid, _ = _split3(h2)
    logits = _dot_nt(r_hi, h_hi) + (_dot_nt(r_hi, h_mid) + _dot_nt(r_mid, h_hi))
    scores = _sigmoid(logits)
    sel = scores + rb_ref[...]

    sub = lax.broadcasted_iota(jnp.int32, (EXPERTS_PER_GROUP, tm), 0).astype(F32)
    neg = -jnp.inf
    best = None
    for g in range(N_EXPERT_GROUPS):
        blk = sel[g * EXPERTS_PER_GROUP:(g + 1) * EXPERTS_PER_GROUP]
        m1, i1 = _first_argmax(blk, sub, EXPERTS_PER_GROUP)
        m2 = jnp.max(jnp.where(sub == i1, neg, blk), axis=0, keepdims=True)
        gs = m1 + m2
        if best is None:
            best, gi = gs, jnp.zeros_like(gs)
            in_sel, in_sc = blk, scores[0:EXPERTS_PER_GROUP]
        else:
            upd = gs > best
            best = jnp.where(upd, gs, best)
            gi = jnp.where(upd, float(g), gi)
            in_sel = jnp.where(upd, blk, in_sel)
            in_sc = jnp.where(upd, scores[g * EXPERTS_PER_GROUP:(g + 1) * EXPERTS_PER_GROUP], in_sc)
    _, l1 = _first_argmax(in_sel, sub, EXPERTS_PER_GROUP)
    _, l2 = _first_argmax(jnp.where(sub == l1, neg, in_sel), sub, EXPERTS_PER_GROUP)
    s1 = jnp.sum(jnp.where(sub == l1, in_sc, 0.0), axis=0, keepdims=True)
    s2 = jnp.sum(jnp.where(sub == l2, in_sc, 0.0), axis=0, keepdims=True)
    e1 = gi * EXPERTS_PER_GROUP + l1
    e2 = gi * EXPERTS_PER_GROUP + l2
    mw_ref[0:1, rs] = s1 / (s1 + s2)
    mw_ref[1:2, rs] = s2 / (s1 + s2)

    eio = lax.broadcasted_iota(jnp.int32, (n_exp, tm), 0).astype(F32)
    oh1 = eio == e1
    oh2 = eio == e2
    onehot = jnp.where(oh1 | oh2, 1.0, 0.0)
    tr = lax.broadcasted_iota(jnp.int32, (tm, tm), 0)
    tc = lax.broadcasted_iota(jnp.int32, (tm, tm), 1)
    before = jnp.where(tr < tc, 1.0, 0.0).astype(BF16)
    pref = _dot(onehot.astype(BF16), before) + base
    r1 = jnp.sum(jnp.where(oh1, pref, 0.0), axis=0, keepdims=True)
    r2 = jnp.sum(jnp.where(oh2, pref, 0.0), axis=0, keepdims=True)
    mi_ref[0:1, rs] = e1.astype(jnp.int32)
    mi_ref[1:2, rs] = e2.astype(jnp.int32)
    mi_ref[2:3, rs] = r1.astype(jnp.int32)
    mi_ref[3:4, rs] = r2.astype(jnp.int32)
    return base + jnp.sum(onehot, axis=1, keepdims=True)


def _out_projection(x2, ya, yb, proj, w_out_b, g1, nw2, sc2, sh2, rwt, rb, seq):
    n, d = x2.shape
    n_exp = rwt.shape[0]
    tm = min(OUT_TM, seq)
    tpb = seq // tm
    bvec = pl.BlockSpec((1, 1, d), lambda i: (i // tpb, 0, 0))
    return pl.pallas_call(
        _outproj_kernel,
        grid=(n // tm,),
        in_specs=[
            pl.BlockSpec((tm, d), lambda i: (i, 0)),
            pl.BlockSpec((tm, d), lambda i: (i, 0)),
            pl.BlockSpec((tm, d), lambda i: (i, 0)),
            pl.BlockSpec((tm, d), lambda i: (i, 6)),
            pl.BlockSpec((tm, d), lambda i: (i, 7)),
            pl.BlockSpec((d, d), lambda i: (0, 0)),
            bvec,
            pl.BlockSpec((1, d), lambda i: (0, 0)),
            bvec, bvec,
            pl.BlockSpec((n_exp, d), lambda i: (0, 0)),
            pl.BlockSpec((n_exp, 1), lambda i: (0, 0)),
        ],
        out_specs=[
            pl.BlockSpec((tm, d), lambda i: (i, 0)),
            pl.BlockSpec((tm, d), lambda i: (i, 0)),
            pl.BlockSpec((4, tm), lambda i: (0, i)),
            pl.BlockSpec((2, tm), lambda i: (0, i)),
            pl.BlockSpec((n_exp, LANES), lambda i: (0, 0)),
        ],
        out_shape=[
            jax.ShapeDtypeStruct((n, d), F32),
            jax.ShapeDtypeStruct((n, d), F32),
            jax.ShapeDtypeStruct((4, n), jnp.int32),
            jax.ShapeDtypeStruct((2, n), F32),
            jax.ShapeDtypeStruct((n_exp, LANES), F32),
        ],
        scratch_shapes=[pltpu.VMEM((n_exp, 1), F32)],
        compiler_params=_cparams(("arbitrary",)),
        name="merge_outproj_router",
    )(x2, ya, yb, proj, proj, w_out_b, g1, nw2, sc2, sh2, rwt, rb)


def _row_copy(src_ref, src_row, dst_ref, dst_row, sem):
    return pltpu.make_async_copy(src_ref.at[pl.ds(src_row, 1)], dst_ref.at[pl.ds(dst_row, 1)], sem)


def _dispatch_kernel(pend_ref, dest_ref, h_ref, xb_ref, zero_scr, sem, zero_sem):
    tm = h_ref.shape[0]
    n_exp = pend_ref.shape[0]

    @pl.when(pl.program_id(0) == 0)
    def _():
        zero_scr[...] = jnp.zeros(zero_scr.shape, F32)

        def clear(row0):
            row0 = pl.multiple_of(row0, MOE_ROWS)
            return pltpu.make_async_copy(zero_scr, xb_ref.at[pl.ds(row0, MOE_ROWS)], zero_sem)

        def last_block(e):
            return clear(jnp.maximum(pend_ref[e] - MOE_ROWS, 0))

        for e in range(n_exp):
            last_block(e).start()
        for e in range(n_exp):
            last_block(e).wait()

        n_used = pend_ref[n_exp - 1] // MOE_ROWS
        n_blocks = xb_ref.shape[0] // MOE_ROWS

        def start_tail(j, carry):
            clear(j * MOE_ROWS).start()
            return carry

        def wait_tail(j, carry):
            clear(j * MOE_ROWS).wait()
            return carry

        lax.fori_loop(n_used, n_blocks, start_tail, 0)
        lax.fori_loop(n_used, n_blocks, wait_tail, 0)

    def start(t, carry):
        _row_copy(h_ref, t, xb_ref, dest_ref[0, t], sem).start(priority=0)
        _row_copy(h_ref, t, xb_ref, dest_ref[1, t], sem).start(priority=1)
        return carry

    lax.fori_loop(0, tm, start, 0, unroll=DMA_UNROLL)
    for _ in range(2):
        pltpu.make_async_copy(h_ref, xb_ref.at[pl.ds(0, tm)], sem).wait()


def _dispatch(pend, dest, h2, n_rows):
    n, d = h2.shape
    tm = min(TOKEN_TILE, n)
    grid_spec = pltpu.PrefetchScalarGridSpec(
        num_scalar_prefetch=1,
        grid=(n // tm,),
        in_specs=[
            pl.BlockSpec((2, tm), lambda i, pe: (0, i), memory_space=pltpu.SMEM),
            pl.BlockSpec((tm, d), lambda i, pe: (i, 0)),
        ],
        out_specs=pl.BlockSpec(memory_space=pl.ANY),
        scratch_shapes=[
            pltpu.VMEM((MOE_ROWS, d), F32),
            pltpu.SemaphoreType.DMA(()),
            pltpu.SemaphoreType.DMA(()),
        ],
    )
    return pl.pallas_call(
        _dispatch_kernel,
        grid_spec=grid_spec,
        out_shape=jax.ShapeDtypeStruct((n_rows, d), F32),
        compiler_params=_cparams(("arbitrary",)),
        name="moe_dispatch",
    )(pend, dest, h2)


def _expert_schedule(padded, n_steps):
    n_exp = padded.shape[0]
    eids = jnp.arange(n_exp, dtype=jnp.int32)
    nblk = padded // MOE_ROWS
    blk0 = (jnp.cumsum(padded) - padded) // MOE_ROWS
    nonempty = nblk > 0
    rank = jnp.cumsum(nonempty.astype(jnp.int32)) - 1
    n_nonempty = rank[-1] + 1
    has_next = nonempty & (rank + 1 < n_nonempty)
    seg_len = jnp.where(nonempty, jnp.maximum(nblk, jnp.where(has_next, W_CHUNKS, 0)), 0)
    seg_end = W_CHUNKS + jnp.cumsum(seg_len)
    seg_start = seg_end - seg_len
    total = seg_end[-1]
    by_rank = jnp.sum(jnp.where(nonempty[None, :] & (rank[None, :] == eids[:, None]), eids[None, :], 0), axis=1)

    s = jnp.arange(n_steps, dtype=jnp.int32)
    in_prologue = s < W_CHUNKS
    active = (s >= W_CHUNKS) & (s < total)
    e = jnp.minimum(jnp.sum((seg_end[None, :] <= s[:, None]).astype(jnp.int32), axis=1), n_exp - 1)
    pick = lambda v: jnp.sum(jnp.where(eids[None, :] == e[:, None], v[None, :], 0), axis=1)
    off = s - pick(seg_start)
    nblk_s, len_s = pick(nblk), pick(seg_len)
    do_compute = active & (off < nblk_s)
    last_block = (total > W_CHUNKS) * (jnp.sum(nblk) - 1)
    x_block = jnp.where(active, pick(blk0) + jnp.minimum(off, nblk_s - 1),
                        jnp.where(in_prologue, 0, last_block))
    do_convert = in_prologue | (active & (pick(has_next.astype(jnp.int32)) > 0) & (off >= len_s - W_CHUNKS))
    conv_no = jnp.maximum(jnp.cumsum(do_convert.astype(jnp.int32)) - 1, 0)
    conv_rank = conv_no // W_CHUNKS
    conv_expert = jnp.sum(jnp.where(eids[None, :] == conv_rank[:, None], by_rank[None, :], 0), axis=1)
    as_i32 = lambda v: v.astype(jnp.int32)
    return (as_i32(x_block), as_i32(do_compute), as_i32(pick(rank) % 2), as_i32(do_convert),
            as_i32(conv_expert), as_i32(conv_no % W_CHUNKS), as_i32(conv_rank % 2))


def _expert_kernel(xblk_ref, compute_ref, cslot_ref, convert_ref, cexp_ref, chunk_ref, vslot_ref,
                   x_ref, wgc_ref, wuc_ref, wdc_ref, o_ref, wg_scr, wu_scr, wd_scr):
    del xblk_ref, cexp_ref
    s = pl.program_id(0)
    ch = wgc_ref.shape[3]

    @pl.when(convert_ref[s] != 0)
    def _():
        slot = vslot_ref[s]
        for c in range(W_CHUNKS):
            @pl.when(chunk_ref[s] == c)
            def _():
                wg_scr[slot, :, c * ch:(c + 1) * ch] = wgc_ref[0, 0].astype(BF16)
                wu_scr[slot, :, c * ch:(c + 1) * ch] = wuc_ref[0, 0].astype(BF16)
                wd_scr[slot, c * ch:(c + 1) * ch, :] = wdc_ref[0, 0].astype(BF16)

    @pl.when(compute_ref[s] != 0)
    def _():
        slot = cslot_ref[s]
        x = x_ref[...].astype(BF16)
        hid = _silu(_dot(x, wg_scr[slot])) * _dot(x, wu_scr[slot])
        o_ref[...] = _dot(hid.astype(BF16), wd_scr[slot])


def _experts(layer, padded, xb, w_gate, w_up, w_down):
    n_rows, d = xb.shape
    n_exp, f = w_gate.shape[1], w_gate.shape[3]
    ch = f // W_CHUNKS
    n_steps = n_rows // MOE_ROWS + W_CHUNKS * (n_exp + 1)
    tables = _expert_schedule(padded, n_steps)
    n_tab = len(tables)
    grid_spec = pltpu.PrefetchScalarGridSpec(
        num_scalar_prefetch=n_tab,
        grid=(n_steps,),
        in_specs=[
            pl.BlockSpec((MOE_ROWS, d), lambda s, xb_, cp, cs, cv, ce, ck, vs: (xb_[s], 0)),
            pl.BlockSpec((1, 1, d, ch), lambda s, xb_, cp, cs, cv, ce, ck, vs: (layer, ce[s], 0, ck[s])),
            pl.BlockSpec((1, 1, d, ch), lambda s, xb_, cp, cs, cv, ce, ck, vs: (layer, ce[s], 0, ck[s])),
            pl.BlockSpec((1, 1, ch, d), lambda s, xb_, cp, cs, cv, ce, ck, vs: (layer, ce[s], ck[s], 0)),
        ],
        out_specs=pl.BlockSpec((MOE_ROWS, d), lambda s, xb_, cp, cs, cv, ce, ck, vs: (xb_[s], 0)),
        scratch_shapes=[
            pltpu.VMEM((2, d, f), BF16),
            pltpu.VMEM((2, d, f), BF16),
            pltpu.VMEM((2, f, d), BF16),
        ],
    )
    return pl.pallas_call(
        _expert_kernel,
        grid_spec=grid_spec,
        out_shape=jax.ShapeDtypeStruct((n_rows, d), F32),
        input_output_aliases={n_tab: 0},
        compiler_params=_cparams(("arbitrary",)),
        name="moe_experts",
    )(*tables, xb, w_gate, w_up, w_down)


def _combine_kernel(final, dest_ref, dest_next_ref, x_ref, wt_ref, g2_ref, fw_ref, yb_ref, o_ref,
                    r0_scr, r1_scr, sems):
    tm = x_ref.shape[0]
    i = pl.program_id(0)
    slot = i % 2

    def gather(idx_ref, s):
        def start(t, carry):
            _row_copy(yb_ref, idx_ref[0, t], r0_scr.at[s], t, sems.at[s]).start(priority=0)
            _row_copy(yb_ref, idx_ref[1, t], r1_scr.at[s], t, sems.at[s]).start(priority=1)
            return carry

        lax.fori_loop(0, tm, start, 0, unroll=DMA_UNROLL)

    @pl.when(i == 0)
    def _():
        gather(dest_ref, slot)

    @pl.when(i + 1 < pl.num_programs(0))
    def _():
        gather(dest_next_ref, 1 - slot)

    pltpu.make_async_copy(yb_ref.at[pl.ds(0, tm)], r0_scr.at[slot], sems.at[slot]).wait()
    pltpu.make_async_copy(yb_ref.at[pl.ds(0, tm)], r1_scr.at[slot], sems.at[slot]).wait()

    wt = wt_ref[...]
    ffn = wt[:, 0:1] * r0_scr[slot] + wt[:, 1:2] * r1_scr[slot]
    xn = x_ref[...] + g2_ref[0] * ffn
    if final:
        ms = jnp.mean(xn * xn, axis=-1, keepdims=True)
        xn = xn * lax.rsqrt(ms + EPS) * fw_ref[...]
    o_ref[...] = xn


def _combine(dest, x2, wt, g2, fw, yb, seq, final):
    n, d = x2.shape
    tm = min(TOKEN_TILE, seq)
    tpb = seq // tm
    last = n // tm - 1
    return pl.pallas_call(
        functools.partial(_combine_kernel, final),
        grid=(n // tm,),
        in_specs=[
            pl.BlockSpec((2, tm), lambda i: (0, i), memory_space=pltpu.SMEM),
            pl.BlockSpec((2, tm), lambda i: (0, jnp.minimum(i + 1, last)), memory_space=pltpu.SMEM),
            pl.BlockSpec((tm, d), lambda i: (i, 0)),
            pl.BlockSpec((tm, 2), lambda i: (i, 0)),
            pl.BlockSpec((1, 1, d), lambda i: (i // tpb, 0, 0)),
            pl.BlockSpec((1, d), lambda i: (0, 0)),
            pl.BlockSpec(memory_space=pl.ANY),
        ],
        out_specs=pl.BlockSpec((tm, d), lambda i: (i, 0)),
        out_shape=jax.ShapeDtypeStruct((n, d), F32),
        scratch_shapes=[
            pltpu.VMEM((2, tm, d), F32),
            pltpu.VMEM((2, tm, d), F32),
            pltpu.SemaphoreType.DMA((2,)),
        ],
        compiler_params=_cparams(("arbitrary",)),
        name="moe_combine",
    )(dest, dest, x2, wt, g2, fw, yb)


def _layer_params(l, d, w_in, conv_w, conv_b, dt_bias, a_log, d_skip, ssm_norm_w,
                  lb_all, hgrn_norm_w):
    gn = SSM_GROUPS * SSM_STATE
    n_heads = d // SSM_HEAD_DIM
    w_r, w_dt = _relayout_w_in(w_in, l)
    pad_h = lambda v: jnp.pad(v.astype(F32), (0, LANES - n_heads)).reshape(1, LANES)
    lb = jnp.maximum(lb_all[l], 0.0)
    head_of_col = jnp.arange(d, dtype=jnp.int32) // SSM_HEAD_DIM
    head_expand = (jnp.arange(LANES, dtype=jnp.int32)[:, None] == head_of_col[None, :]).astype(BF16)
    cw, cb = conv_w[l], conv_b[l]
    return dict(
        w_r=w_r, w_dt=w_dt,
        cwx=cw[:, :d], cwb=cw[:, d:d + gn], cwc=cw[:, d + gn:],
        cbx=cb[:d].reshape(1, d), cbb=cb[d:d + gn].reshape(1, gn), cbc=cb[d + gn:].reshape(1, gn),
        dt_bias=pad_h(dt_bias[l]), a_log=pad_h(a_log[l]),
        d_skip=jnp.repeat(d_skip[l].astype(F32), SSM_HEAD_DIM).reshape(1, d),
        ssm_norm_w=ssm_norm_w[l].reshape(1, d),
        head_expand=head_expand,
        log_lb=(jnp.log(lb) * LOG2E).reshape(1, d), log_1m_lb=(jnp.log1p(-lb) * LOG2E).reshape(1, d),
        hgrn_norm_w=hgrn_norm_w[l].reshape(1, d),
    )


def kernel(x, c, ada_w, ada_b, norm1_w, norm2_w, w_in, conv_w, conv_b, dt_bias, a_log, d_skip,
           ssm_norm_w, hgrn_lower_bounds, hgrn_norm_w, w_out, router_w, router_bias,
           w_gate, w_up, w_down, final_norm_w):
    bsz, seq, d = x.shape
    n_layers = ada_w.shape[0]
    n_exp = router_w.shape[1]
    n = bsz * seq
    n_assign = 2 * n
    nb = n_assign // MOE_ROWS + n_exp
    n_rows = nb * MOE_ROWS

    lb_all = jnp.cumsum(jax.nn.softmax(hgrn_lower_bounds.astype(F32), axis=0), axis=0)
    lb_all = lb_all - lb_all[0:1]
    mod = _modulation(c, ada_w, ada_b)
    rwt = router_w.astype(F32).T
    rb = router_bias.astype(F32).reshape(n_exp, 1)
    fw = final_norm_w.reshape(1, d)
    w_in = w_in.astype(BF16)

    x2 = x.reshape(n, d)
    for l in range(n_layers):
        sh1, sc1, g1, sh2, sc2, g2 = [mod[l, :, k * d:(k + 1) * d].reshape(bsz, 1, d) for k in range(6)]
        p = _layer_params(l, d, w_in, conv_w, conv_b, dt_bias, a_log, d_skip, ssm_norm_w,
                          lb_all, hgrn_norm_w)
        proj, dt_raw = _in_projection(x2, norm1_w[l].reshape(1, d), sc1, sh1, p["w_r"], p["w_dt"], seq)
        ya = _ssd_branch(proj, dt_raw, p, bsz, seq, d)
        yb = _hgrn_branch(proj, p, bsz, seq, d)
        x2, h2, mi, mw, cnt = _out_projection(
            x2, ya, yb, proj, w_out[l].astype(BF16), g1, norm2_w[l].reshape(1, d), sc2, sh2, rwt, rb, seq)

        counts = cnt[:, 0].astype(jnp.int32)
        padded = (counts + MOE_ROWS - 1) // MOE_ROWS * MOE_ROWS
        pend = jnp.cumsum(padded)
        pstart = pend - padded
        is_e = mi[0:2, :, None] == jnp.arange(n_exp, dtype=jnp.int32)
        dest = jnp.sum(jnp.where(is_e, pstart, 0), axis=-1) + mi[2:4]

        xb = _dispatch(pend.astype(jnp.int32), dest, h2, n_rows)
        yexp = _experts(l, padded, xb, w_gate, w_up, w_down)
        x2 = _combine(dest, x2, mw.T, g2, fw, yexp, seq, final=(l == n_layers - 1))
    return x2.reshape(bsz, seq, d)
```

```python
import functools

import jax
import jax.numpy as jnp
from jax import lax
from jax.experimental import pallas as pl
from jax.experimental.pallas import tpu as pltpu

F32 = jnp.float32
BF16 = jnp.bfloat16
EPS = 1e-6
LOG2E = 1.4426950408889634

SSM_HEAD_DIM = 64
SSM_STATE = 128
SSM_GROUPS = 4
CONV_K = 4
HGRN_HEAD = 128
N_EXPERT_GROUPS = 4
EXPERTS_PER_GROUP = 8
LANES = 128
SUBLANES = 8

SSD_CHUNK = 128
SSD_CHUNKS_PER_STEP = 4
HGRN_CHUNK = 128
HGRN_HEADS_PER_STEP = 8
HGRN_CHUNKS_PER_STEP = 4
MOE_ROWS = 256
W_CHUNKS = 4
INPROJ_TM = 1024
INPROJ_TN = 1024
OUT_TM = 256
ROUTE_ROWS = 128
TOKEN_TILE = 256
DMA_UNROLL = 32
VMEM_LIMIT = 56 * 1024 * 1024


def _cparams(sem):
    return pltpu.CompilerParams(dimension_semantics=sem, vmem_limit_bytes=VMEM_LIMIT)


def _sigmoid(x):
    return 1.0 / (1.0 + jnp.exp(-x))


def _silu(x):
    return x * _sigmoid(x)


def _split3(x):
    hi = x.astype(BF16)
    r1 = x - hi.astype(F32)
    mid = r1.astype(BF16)
    lo = (r1 - mid.astype(F32)).astype(BF16)
    return hi, mid, lo


def _dot(a, b):
    return jnp.dot(a, b, preferred_element_type=F32)


def _dot_nt(a, b):
    return lax.dot_general(a, b, (((1,), (1,)), ((), ())), preferred_element_type=F32)


def _sel_dot(sel_bf16, x):
    hi, mid, lo = _split3(x)
    return _dot(sel_bf16, hi) + (_dot(sel_bf16, mid) + _dot(sel_bf16, lo))


def _dot_sel(x, sel_bf16):
    hi, mid, lo = _split3(x)
    return _dot(hi, sel_bf16) + (_dot(mid, sel_bf16) + _dot(lo, sel_bf16))


def _tril_ones(n):
    r = lax.broadcasted_iota(jnp.int32, (n, n), 0)
    c = lax.broadcasted_iota(jnp.int32, (n, n), 1)
    return jnp.where(r >= c, 1.0, 0.0).astype(BF16)


def _mod_kernel(c_ref, w_ref, b_ref, o_ref):
    ca = _silu(c_ref[...]).astype(BF16)
    o_ref[0] = _dot(ca, w_ref[0].astype(BF16)) + b_ref[0]


def _modulation(c, ada_w, ada_b):
    n_layers, d, n6 = ada_w.shape
    bsz = c.shape[0]
    tn = 1024
    return pl.pallas_call(
        _mod_kernel,
        grid=(n_layers, n6 // tn),
        in_specs=[
            pl.BlockSpec((bsz, d), lambda l, j: (0, 0)),
            pl.BlockSpec((1, d, tn), lambda l, j: (l, 0, j)),
            pl.BlockSpec((1, 1, tn), lambda l, j: (l, 0, j)),
        ],
        out_specs=pl.BlockSpec((1, bsz, tn), lambda l, j: (l, 0, j)),
        out_shape=jax.ShapeDtypeStruct((n_layers, bsz, n6), F32),
        compiler_params=_cparams(("arbitrary", "arbitrary")),
        name="adaln_mod",
    )(c, ada_w, ada_b.reshape(n_layers, 1, n6))


_DIRECT, _LOAD_ONLY, _SHIFTED = 0, 1, 2


def _relayout_kernel(shift, src_ref, dst_ref, mode_ref, w_ref, o_ref, odt_ref, prev_scr):
    del src_ref, dst_ref
    t = pl.program_id(0)
    mode = mode_ref[t]

    @pl.when(mode == _DIRECT)
    def _():
        o_ref[...] = w_ref[0].astype(BF16)

    @pl.when(mode == _LOAD_ONLY)
    def _():
        head = w_ref[0, :, :LANES]
        lane = lax.broadcasted_iota(jnp.int32, head.shape, 1)
        odt_ref[...] = jnp.where(lane < shift, head, 0.0).astype(BF16)
        prev_scr[...] = w_ref[0]

    @pl.when(mode == _SHIFTED)
    def _():
        o_ref[...] = jnp.concatenate([prev_scr[:, shift:], w_ref[0, :, :shift]], axis=1).astype(BF16)
        prev_scr[...] = w_ref[0]


def _relayout_w_in(w_in, layer):
    d = w_in.shape[1]
    gn2 = 2 * SSM_GROUPS * SSM_STATE
    n_heads = d // SSM_HEAD_DIM
    tn = INPROJ_TN
    assert gn2 == tn and (2 * d) % tn == 0 and 0 < n_heads < LANES
    n_front, n_rest = 2 * d // tn, 6 * d // tn
    src = list(range(n_front)) + [n_front + 1 + k for k in range(n_rest + 1)] + [n_front]
    dst = list(range(n_front)) + [n_front - 1] + [n_front + k for k in range(n_rest)] + [n_front + n_rest]
    mode = [_DIRECT] * n_front + [_LOAD_ONLY] + [_SHIFTED] * n_rest + [_DIRECT]
    n_tiles = n_front + n_rest + 1
    grid_spec = pltpu.PrefetchScalarGridSpec(
        num_scalar_prefetch=3,
        grid=(len(src),),
        in_specs=[pl.BlockSpec((1, d, tn), lambda t, s, o, m: (layer, 0, s[t]))],
        out_specs=[
            pl.BlockSpec((d, tn), lambda t, s, o, m: (0, o[t])),
            pl.BlockSpec((d, LANES), lambda t, s, o, m: (0, 0)),
        ],
        scratch_shapes=[pltpu.VMEM((d, tn), w_in.dtype)],
    )
    return pl.pallas_call(
        functools.partial(_relayout_kernel, n_heads),
        grid_spec=grid_spec,
        out_shape=[
            jax.ShapeDtypeStruct((d, n_tiles * tn), BF16),
            jax.ShapeDtypeStruct((d, LANES), BF16),
        ],
        compiler_params=_cparams(("arbitrary",)),
        name="w_in_relayout",
    )(jnp.asarray(src, jnp.int32), jnp.asarray(dst, jnp.int32), jnp.asarray(mode, jnp.int32), w_in)


def _rms_mod(x, nw, sc, sh):
    ms = jnp.mean(x * x, axis=-1, keepdims=True)
    return (x * lax.rsqrt(ms + EPS) * nw) * (1.0 + sc) + sh


def _inproj_kernel(x_ref, nw_ref, sc_ref, sh_ref, w_ref, wdt_ref, o_ref, dt_ref, h_scr):
    @pl.when(pl.program_id(1) == 0)
    def _():
        h = _rms_mod(x_ref[...], nw_ref[...], sc_ref[0], sh_ref[0]).astype(BF16)
        h_scr[...] = h
        dt_ref[...] = _dot(h, wdt_ref[...])

    o_ref[...] = _dot(h_scr[...], w_ref[...])


def _in_projection(x2, nw, sc, sh, w_r, w_dt, seq):
    n, d = x2.shape
    np_ = w_r.shape[1]
    tm = min(INPROJ_TM, seq)
    tn = INPROJ_TN
    tiles_per_batch = seq // tm
    return pl.pallas_call(
        _inproj_kernel,
        grid=(n // tm, np_ // tn),
        in_specs=[
            pl.BlockSpec((tm, d), lambda i, j: (i, 0)),
            pl.BlockSpec((1, d), lambda i, j: (0, 0)),
            pl.BlockSpec((1, 1, d), lambda i, j: (i // tiles_per_batch, 0, 0)),
            pl.BlockSpec((1, 1, d), lambda i, j: (i // tiles_per_batch, 0, 0)),
            pl.BlockSpec((d, tn), lambda i, j: (0, j)),
            pl.BlockSpec((d, LANES), lambda i, j: (0, 0)),
        ],
        out_specs=[
            pl.BlockSpec((tm, tn), lambda i, j: (i, j)),
            pl.BlockSpec((tm, LANES), lambda i, j: (i, 0)),
        ],
        out_shape=[
            jax.ShapeDtypeStruct((n, np_), F32),
            jax.ShapeDtypeStruct((n, LANES), F32),
        ],
        scratch_shapes=[pltpu.VMEM((tm, d), BF16)],
        compiler_params=_cparams(("arbitrary", "arbitrary")),
        name="norm_inproj",
    )(x2, nw, sc, sh, w_r, w_dt)


def _causal_conv(ext_ref, u, w_ref, b_ref, first):
    q = u.shape[0]

    if first is not None:
        @pl.when(first)
        def _():
            ext_ref[0:SUBLANES, :] = jnp.zeros((SUBLANES, u.shape[1]), F32)

    ext_ref[SUBLANES:SUBLANES + q, :] = u
    acc = b_ref[...] + w_ref[CONV_K - 1:CONV_K, :] * u
    for k in range(CONV_K - 1):
        off = SUBLANES - (CONV_K - 1) + k
        acc = acc + w_ref[k:k + 1, :] * ext_ref[off:off + q, :]
    ext_ref[0:SUBLANES, :] = ext_ref[q:q + SUBLANES, :]
    return _silu(acc)


def _ssd_kernel(z_ref, xs_ref, bm_ref, cm_ref, dt_ref,
                cwx_ref, cwb_ref, cwc_ref, cbx_ref, cbb_ref, cbc_ref,
                dtb_ref, alog_ref, dskip_ref, nw_ref, exp_ref,
                o_ref,
                state_scr, extx_scr, extb_scr, extc_scr):
    rows = z_ref.shape[0]
    q = min(SSD_CHUNK, rows)
    first = pl.program_id(1) == 0

    @pl.when(first)
    def _():
        state_scr[...] = jnp.zeros(state_scr.shape, F32)

    for k in range(rows // q):
        _ssd_chunk(slice(k * q, (k + 1) * q), first if k == 0 else None,
                   z_ref, xs_ref, bm_ref, cm_ref, dt_ref,
                   cwx_ref, cwb_ref, cwc_ref, cbx_ref, cbb_ref, cbc_ref,
                   dtb_ref, alog_ref, dskip_ref, nw_ref, exp_ref,
                   o_ref, state_scr, extx_scr, extb_scr, extc_scr)


def _ssd_chunk(rs, first, z_ref, xs_ref, bm_ref, cm_ref, dt_ref,
               cwx_ref, cwb_ref, cwc_ref, cbx_ref, cbb_ref, cbc_ref,
               dtb_ref, alog_ref, dskip_ref, nw_ref, exp_ref,
               o_ref, state_scr, extx_scr, extb_scr, extc_scr):
    q = rs.stop - rs.start
    d_inner = z_ref.shape[1]
    gw = d_inner // SSM_GROUPS
    heads_per_group = gw // SSM_HEAD_DIM

    xs = _causal_conv(extx_scr, xs_ref[rs, :], cwx_ref, cbx_ref, first)
    bmat = _causal_conv(extb_scr, bm_ref[rs, :], cwb_ref, cbb_ref, first)
    cmat = _causal_conv(extc_scr, cm_ref[rs, :], cwc_ref, cbc_ref, first)

    x_dt = dt_ref[rs, :] + dtb_ref[...]
    dt = jnp.maximum(x_dt, 0.0) + jnp.log(1.0 + jnp.exp(-jnp.abs(x_dt)))
    a = -jnp.exp(alog_ref[...])
    acs = _sel_dot(_tril_ones(q), dt * a)
    acs_last = acs[q - 1:q, :]
    stack = jnp.concatenate(
        [dt, jnp.exp(acs), jnp.exp(acs_last - acs),
         jnp.broadcast_to(jnp.exp(acs_last), (SUBLANES, LANES))], axis=0)
    wide = _dot_sel(stack, exp_ref[...])
    dt_e = wide[0:q]
    eacs_e = wide[q:2 * q]
    dec_e = wide[2 * q:3 * q]
    cd_e = wide[3 * q:3 * q + 1]

    xdt = xs * dt_e
    xdt_b = xdt.astype(BF16)
    xdec_b = (xdt * dec_e).astype(BF16)

    acs_t = acs.T
    ri = lax.broadcasted_iota(jnp.int32, (q, q), 0)
    ci = lax.broadcasted_iota(jnp.int32, (q, q), 1)
    causal = ri >= ci
    lane = lax.broadcasted_iota(jnp.int32, (q, LANES), 1)
    lo_half = lane < SSM_HEAD_DIM

    y_parts = []
    for g in range(SSM_GROUPS):
        bg = bmat[:, g * SSM_STATE:(g + 1) * SSM_STATE]
        cg = cmat[:, g * SSM_STATE:(g + 1) * SSM_STATE]
        bg_b = bg.astype(BF16)
        cg_b = cg.astype(BF16)
        cb = _dot_nt(cg_b, bg_b)
        st = state_scr[g]
        y_off = _dot(cg_b, st.astype(BF16)) * eacs_e[:, g * gw:(g + 1) * gw]
        diag = []
        for pair in range(heads_per_group // 2):
            ms = []
            for sub in range(2):
                h = g * heads_per_group + pair * 2 + sub
                rel = acs[:, h:h + 1] - acs_t[h:h + 1, :]
                lmat = jnp.exp(jnp.where(causal, rel, -jnp.inf))
                ms.append((cb * lmat).astype(BF16))
            lhs = jnp.concatenate(ms, axis=1)
            c0 = (g * heads_per_group + pair * 2) * SSM_HEAD_DIM
            xp = xdt_b[:, c0:c0 + LANES]
            zero = jnp.zeros_like(xp)
            rhs = jnp.concatenate(
                [jnp.where(lo_half, xp, zero), jnp.where(lo_half, zero, xp)], axis=0)
            diag.append(_dot(lhs, rhs))
        y_parts.append(jnp.concatenate(diag, axis=1) + y_off)
        new_state = st * cd_e[:, g * gw:(g + 1) * gw] + _dot(
            bg.T.astype(BF16), xdec_b[:, g * gw:(g + 1) * gw])
        state_scr[g] = new_state

    y = jnp.concatenate(y_parts, axis=1) + dskip_ref[...] * xs
    y = y * _silu(z_ref[rs, :])
    outs = []
    for g in range(SSM_GROUPS):
        yg = y[:, g * gw:(g + 1) * gw]
        ms = jnp.mean(yg * yg, axis=-1, keepdims=True)
        outs.append(yg * lax.rsqrt(ms + EPS))
    o_ref[rs, :] = (jnp.concatenate(outs, axis=1) * nw_ref[...]).astype(o_ref.dtype)


def _ssd_branch(proj, dt_raw, p, bsz, seq, d):
    chunk = min(SSD_CHUNK, seq)
    q = min(SSD_CHUNK * SSD_CHUNKS_PER_STEP, seq)
    nc = seq // q
    gn = SSM_GROUPS * SSM_STATE
    b_blk = 8 * d // gn
    row = lambda b, c: b * nc + c
    vec = lambda width: pl.BlockSpec((1, width), lambda b, c: (0, 0))
    return pl.pallas_call(
        _ssd_kernel,
        grid=(bsz, nc),
        in_specs=[
            pl.BlockSpec((q, d), lambda b, c: (row(b, c), 0)),
            pl.BlockSpec((q, d), lambda b, c: (row(b, c), 1)),
            pl.BlockSpec((q, gn), lambda b, c: (row(b, c), b_blk)),
            pl.BlockSpec((q, gn), lambda b, c: (row(b, c), b_blk + 1)),
            pl.BlockSpec((q, LANES), lambda b, c: (row(b, c), 0)),
            pl.BlockSpec((CONV_K, d), lambda b, c: (0, 0)),
            pl.BlockSpec((CONV_K, gn), lambda b, c: (0, 0)),
            pl.BlockSpec((CONV_K, gn), lambda b, c: (0, 0)),
            vec(d), vec(gn), vec(gn),
            vec(LANES), vec(LANES), vec(d), vec(d),
            pl.BlockSpec((LANES, d), lambda b, c: (0, 0)),
        ],
        out_specs=pl.BlockSpec((q, d), lambda b, c: (row(b, c), 0)),
        out_shape=jax.ShapeDtypeStruct((bsz * seq, d), BF16),
        scratch_shapes=[
            pltpu.VMEM((SSM_GROUPS, SSM_STATE, d // SSM_GROUPS), F32),
            pltpu.VMEM((chunk + SUBLANES, d), F32),
            pltpu.VMEM((chunk + SUBLANES, gn), F32),
            pltpu.VMEM((chunk + SUBLANES, gn), F32),
        ],
        compiler_params=_cparams(("arbitrary", "arbitrary")),
        name="ssd_scan",
    )(proj, proj, proj, proj, dt_raw,
      p["cwx"], p["cwb"], p["cwc"], p["cbx"], p["cbb"], p["cbc"],
      p["dt_bias"], p["a_log"], p["d_skip"], p["ssm_norm_w"], p["head_expand"])


def _hgrn_chunk(q_raw, f_raw, v, g_raw, loglb, log1mlb, nw, s_t, tril, xor_idx, below):
    c = q_raw.shape[0]
    nv = c // SUBLANES
    f2 = f_raw * LOG2E
    logsig = jnp.minimum(f2, 0.0) - jnp.log2(1.0 + jnp.exp2(-jnp.abs(f2)))
    bb = log1mlb + logsig
    logf2 = jnp.maximum(loglb, bb) + jnp.log2(1.0 + jnp.exp2(-jnp.abs(loglb - bb)))
    kk = jnp.exp2(bb - f2)
    qf = _silu(q_raw)
    b = _sel_dot(tril, logf2)

    b3 = b.reshape(nv, SUBLANES, LANES)
    q3 = qf.reshape(nv, SUBLANES, LANES)
    k3 = kk.reshape(nv, SUBLANES, LANES)
    sub = lax.broadcasted_iota(jnp.int32, (1, SUBLANES, LANES), 1)

    def brow(i):
        return jnp.broadcast_to(b3[:, i:i + 1, :], b3.shape)

    levels = []
    up = (sub & 1) != 0
    levels.append((1, jnp.where(up, q3 * jnp.exp2(logf2.reshape(b3.shape)), k3).reshape(c, LANES)))
    up = (sub & 2) != 0
    d = (b3 - jnp.where(sub < 4, brow(1), brow(5))) * jnp.where(up, 1.0, -1.0)
    levels.append((2, (jnp.where(up, q3, k3) * jnp.exp2(d)).reshape(c, LANES)))
    up = sub >= 4
    d = (b3 - brow(3)) * jnp.where(up, 1.0, -1.0)
    levels.append((4, (jnp.where(up, q3, k3) * jnp.exp2(d)).reshape(c, LANES)))
    w = SUBLANES
    while w < c:
        parts = []
        for j in range(c // (2 * w)):
            lo, mid, hi = 2 * w * j, 2 * w * j + w, 2 * w * (j + 1)
            bm = b[mid - 1:mid, :]
            parts.append(kk[lo:mid] * jnp.exp2(bm - b[lo:mid]))
            parts.append(qf[mid:hi] * jnp.exp2(b[mid:hi] - bm))
        levels.append((w, jnp.concatenate(parts, axis=0)))
        w *= 2

    scores = None
    for w, xw in reversed(levels):
        xb = xw.astype(BF16)
        a_w = _dot_nt(xb, xb)
        scores = a_w if scores is None else jnp.where(xor_idx < 2 * w, a_w, scores)
    scores = jnp.where(below, scores, 0.0)

    o = _dot(scores.astype(BF16), v.astype(BF16))
    o = o + jnp.sum(qf * kk, axis=-1, keepdims=True) * v
    o = o + _dot_nt((qf * jnp.exp2(b)).astype(BF16), s_t.astype(BF16))
    b_last = b[c - 1:c, :]
    kst = (kk * jnp.exp2(b_last - b)).astype(BF16)
    s_new = s_t * jnp.exp2(b_last) + _dot(v.T.astype(BF16), kst)

    ms = jnp.mean(o * o, axis=-1, keepdims=True)
    y = o * lax.rsqrt(ms + EPS) * nw * _silu(g_raw)
    return y, s_new


def _hgrn_kernel(q_ref, f_ref, i_ref, g_ref, loglb_ref, log1mlb_ref, nw_ref, o_ref, state_scr):
    rows = q_ref.shape[0]
    heads = q_ref.shape[1] // HGRN_HEAD
    c = min(HGRN_CHUNK, rows)

    @pl.when(pl.program_id(2) == 0)
    def _():
        state_scr[...] = jnp.zeros(state_scr.shape, F32)

    tril = _tril_ones(c)
    ri = lax.broadcasted_iota(jnp.int32, (c, c), 0)
    ci = lax.broadcasted_iota(jnp.int32, (c, c), 1)
    xor_idx = ri ^ ci
    below = ri > ci
    for h in range(heads):
        cols = slice(h * HGRN_HEAD, (h + 1) * HGRN_HEAD)
        s_t = state_scr[h]
        for k in range(rows // c):
            rs = slice(k * c, (k + 1) * c)
            y, s_t = _hgrn_chunk(q_ref[rs, cols], f_ref[rs, cols], i_ref[rs, cols], g_ref[rs, cols],
                                 loglb_ref[:, cols], log1mlb_ref[:, cols], nw_ref[:, cols], s_t,
                                 tril, xor_idx, below)
            o_ref[rs, cols] = y.astype(o_ref.dtype)
        state_scr[h] = s_t


def _hgrn_branch(proj, p, bsz, seq, d):
    n_heads = d // HGRN_HEAD
    hb = min(HGRN_HEADS_PER_STEP, n_heads)
    wblk = hb * HGRN_HEAD
    rows = min(HGRN_CHUNK * HGRN_CHUNKS_PER_STEP, seq)
    nt = seq // rows
    per = d // wblk

    def sect(k):
        return pl.BlockSpec((rows, wblk), lambda b, hg, t: (b * nt + t, k * per + hg))

    vec = pl.BlockSpec((1, wblk), lambda b, hg, t: (0, hg))
    return pl.pallas_call(
        _hgrn_kernel,
        grid=(bsz, n_heads // hb, nt),
        in_specs=[sect(2), sect(3), sect(4), sect(5), vec, vec, vec],
        out_specs=pl.BlockSpec((rows, wblk), lambda b, hg, t: (b * nt + t, hg)),
        out_shape=jax.ShapeDtypeStruct((bsz * seq, d), BF16),
        scratch_shapes=[pltpu.VMEM((hb, HGRN_HEAD, HGRN_HEAD), F32)],
        compiler_params=_cparams(("arbitrary", "arbitrary", "arbitrary")),
        name="hgrn2_scan",
    )(proj, proj, proj, proj, p["log_lb"], p["log_1m_lb"], p["hgrn_norm_w"])


def _first_argmax(vals, idx, n):
    m = jnp.max(vals, axis=0, keepdims=True)
    i = jnp.min(jnp.where(vals == m, idx, float(n)), axis=0, keepdims=True)
    return m, i


def _outproj_kernel(x_ref, ya_ref, yb_ref, ga_ref, gb_ref, wout_ref, g1_ref,
                    nw2_ref, sc2_ref, sh2_ref, rwt_ref, rb_ref,
                    xo_ref, h2_ref, mi_ref, mw_ref, cnt_ref, cnt_scr):
    tm = x_ref.shape[0]

    @pl.when(pl.program_id(0) == 0)
    def _():
        cnt_scr[...] = jnp.zeros(cnt_scr.shape, F32)

    merged = (_sigmoid(ga_ref[...]).astype(BF16) * ya_ref[...]
              + _sigmoid(gb_ref[...]).astype(BF16) * yb_ref[...])
    xo_ref[...] = x_ref[...] + g1_ref[0] * _dot(merged, wout_ref[...])

    r_hi, r_mid, _ = _split3(rwt_ref[...])
    ts = min(ROUTE_ROWS, tm)
    total = cnt_scr[...]
    for s in range(tm // ts):
        total = _outproj_rows(slice(s * ts, (s + 1) * ts), total, (r_hi, r_mid),
                              nw2_ref, sc2_ref, sh2_ref, rb_ref, xo_ref, h2_ref, mi_ref, mw_ref)
    cnt_scr[...] = total
    cnt_ref[...] = jnp.broadcast_to(total, cnt_ref.shape)


def _outproj_rows(rs, base, r_split, nw2_ref, sc2_ref, sh2_ref, rb_ref, xo_ref, h2_ref, mi_ref, mw_ref):
    tm = rs.stop - rs.start
    n_exp = rb_ref.shape[0]
    r_hi, r_mid = r_split
    h2 = _rms_mod(xo_ref[rs, :], nw2_ref[...], sc2_ref[0], sh2_ref[0])
    h2_ref[rs, :] = h2

    h_hi, h_mid, _ = _split3(h2)
    logits = _dot_nt(r_hi, h_hi) + (_dot_nt(r_hi, h_mid) + _dot_nt(r_mid, h_hi))
    scores = _sigmoid(logits)
    sel = scores + rb_ref[...]

    sub = lax.broadcasted_iota(jnp.int32, (EXPERTS_PER_GROUP, tm), 0).astype(F32)
    neg = -jnp.inf
    best = None
    for g in range(N_EXPERT_GROUPS):
        blk = sel[g * EXPERTS_PER_GROUP:(g + 1) * EXPERTS_PER_GROUP]
        m1, i1 = _first_argmax(blk, sub, EXPERTS_PER_GROUP)
        m2 = jnp.max(jnp.where(sub == i1, neg, blk), axis=0, keepdims=True)
        gs = m1 + m2
        if best is None:
            best, gi = gs, jnp.zeros_like(gs)
            in_sel, in_sc = blk, scores[0:EXPERTS_PER_GROUP]
        else:
            upd = gs > best
            best = jnp.where(upd, gs, best)
            gi = jnp.where(upd, float(g), gi)
            in_sel = jnp.where(upd, blk, in_sel)
            in_sc = jnp.where(upd, scores[g * EXPERTS_PER_GROUP:(g + 1) * EXPERTS_PER_GROUP], in_sc)
    _, l1 = _first_argmax(in_sel, sub, EXPERTS_PER_GROUP)
    _, l2 = _first_argmax(jnp.where(sub == l1, neg, in_sel), sub, EXPERTS_PER_GROUP)
    s1 = jnp.sum(jnp.where(sub == l1, in_sc, 0.0), axis=0, keepdims=True)
    s2 = jnp.sum(jnp.where(sub == l2, in_sc, 0.0), axis=0, keepdims=True)
    e1 = gi * EXPERTS_PER_GROUP + l1
    e2 = gi * EXPERTS_PER_GROUP + l2
    mw_ref[0:1, rs] = s1 / (s1 + s2)
    mw_ref[1:2, rs] = s2 / (s1 + s2)

    eio = lax.broadcasted_iota(jnp.int32, (n_exp, tm), 0).astype(F32)
    oh1 = eio == e1
    oh2 = eio == e2
    onehot = jnp.where(oh1 | oh2, 1.0, 0.0)
    tr = lax.broadcasted_iota(jnp.int32, (tm, tm), 0)
    tc = lax.broadcasted_iota(jnp.int32, (tm, tm), 1)
    before = jnp.where(tr < tc, 1.0, 0.0).astype(BF16)
    pref = _dot(onehot.astype(BF16), before) + base
    r1 = jnp.sum(jnp.where(oh1, pref, 0.0), axis=0, keepdims=True)
    r2 = jnp.sum(jnp.where(oh2, pref, 0.0), axis=0, keepdims=True)
    mi_ref[0:1, rs] = e1.astype(jnp.int32)
    mi_ref[1:2, rs] = e2.astype(jnp.int32)
    mi_ref[2:3, rs] = r1.astype(jnp.int32)
    mi_ref[3:4, rs] = r2.astype(jnp.int32)
    return base + jnp.sum(onehot, axis=1, keepdims=True)


def _out_projection(x2, ya, yb, proj, w_out_b, g1, nw2, sc2, sh2, rwt, rb, seq):
    n, d = x2.shape
    n_exp = rwt.shape[0]
    tm = min(OUT_TM, seq)
    tpb = seq // tm
    bvec = pl.BlockSpec((1, 1, d), lambda i: (i // tpb, 0, 0))
    return pl.pallas_call(
        _outproj_kernel,
        grid=(n // tm,),
        in_specs=[
            pl.BlockSpec((tm, d), lambda i: (i, 0)),
            pl.BlockSpec((tm, d), lambda i: (i, 0)),
            pl.BlockSpec((tm, d), lambda i: (i, 0)),
            pl.BlockSpec((tm, d), lambda i: (i, 6)),
            pl.BlockSpec((tm, d), lambda i: (i, 7)),
            pl.BlockSpec((d, d), lambda i: (0, 0)),
            bvec,
            pl.BlockSpec((1, d), lambda i: (0, 0)),
            bvec, bvec,
            pl.BlockSpec((n_exp, d), lambda i: (0, 0)),
            pl.BlockSpec((n_exp, 1), lambda i: (0, 0)),
        ],
        out_specs=[
            pl.BlockSpec((tm, d), lambda i: (i, 0)),
            pl.BlockSpec((tm, d), lambda i: (i, 0)),
            pl.BlockSpec((4, tm), lambda i: (0, i)),
            pl.BlockSpec((2, tm), lambda i: (0, i)),
            pl.BlockSpec((n_exp, LANES), lambda i: (0, 0)),
        ],
        out_shape=[
            jax.ShapeDtypeStruct((n, d), F32),
            jax.ShapeDtypeStruct((n, d), F32),
            jax.ShapeDtypeStruct((4, n), jnp.int32),
            jax.ShapeDtypeStruct((2, n), F32),
            jax.ShapeDtypeStruct((n_exp, LANES), F32),
        ],
        scratch_shapes=[pltpu.VMEM((n_exp, 1), F32)],
        compiler_params=_cparams(("arbitrary",)),
        name="merge_outproj_router",
    )(x2, ya, yb, proj, proj, w_out_b, g1, nw2, sc2, sh2, rwt, rb)


def _row_copy(src_ref, src_row, dst_ref, dst_row, sem):
    return pltpu.make_async_copy(src_ref.at[pl.ds(src_row, 1)], dst_ref.at[pl.ds(dst_row, 1)], sem)


def _dispatch_kernel(pend_ref, dest_ref, h_ref, xb_ref, zero_scr, sem, zero_sem):
    tm = h_ref.shape[0]
    n_exp = pend_ref.shape[0]

    @pl.when(pl.program_id(0) == 0)
    def _():
        zero_scr[...] = jnp.zeros(zero_scr.shape, F32)

        def clear(row0):
            row0 = pl.multiple_of(row0, MOE_ROWS)
            return pltpu.make_async_copy(zero_scr, xb_ref.at[pl.ds(row0, MOE_ROWS)], zero_sem)

        def last_block(e):
            return clear(jnp.maximum(pend_ref[e] - MOE_ROWS, 0))

        for e in range(n_exp):
            last_block(e).start()
        for e in range(n_exp):
            last_block(e).wait()

        n_used = pend_ref[n_exp - 1] // MOE_ROWS
        n_blocks = xb_ref.shape[0] // MOE_ROWS

        def start_tail(j, carry):
            clear(j * MOE_ROWS).start()
            return carry

        def wait_tail(j, carry):
            clear(j * MOE_ROWS).wait()
            return carry

        lax.fori_loop(n_used, n_blocks, start_tail, 0)
        lax.fori_loop(n_used, n_blocks, wait_tail, 0)

    def start(t, carry):
        _row_copy(h_ref, t, xb_ref, dest_ref[0, t], sem).start(priority=0)
        _row_copy(h_ref, t, xb_ref, dest_ref[1, t], sem).start(priority=1)
        return carry

    lax.fori_loop(0, tm, start, 0, unroll=DMA_UNROLL)
    for _ in range(2):
        pltpu.make_async_copy(h_ref, xb_ref.at[pl.ds(0, tm)], sem).wait()


def _dispatch(pend, dest, h2, n_rows):
    n, d = h2.shape
    tm = min(TOKEN_TILE, n)
    grid_spec = pltpu.PrefetchScalarGridSpec(
        num_scalar_prefetch=1,
        grid=(n // tm,),
        in_specs=[
            pl.BlockSpec((2, tm), lambda i, pe: (0, i), memory_space=pltpu.SMEM),
            pl.BlockSpec((tm, d), lambda i, pe: (i, 0)),
        ],
        out_specs=pl.BlockSpec(memory_space=pl.ANY),
        scratch_shapes=[
            pltpu.VMEM((MOE_ROWS, d), F32),
            pltpu.SemaphoreType.DMA(()),
            pltpu.SemaphoreType.DMA(()),
        ],
    )
    return pl.pallas_call(
        _dispatch_kernel,
        grid_spec=grid_spec,
        out_shape=jax.ShapeDtypeStruct((n_rows, d), F32),
        compiler_params=_cparams(("arbitrary",)),
        name="moe_dispatch",
    )(pend, dest, h2)


def _expert_schedule(padded, n_steps):
    n_exp = padded.shape[0]
    eids = jnp.arange(n_exp, dtype=jnp.int32)
    nblk = padded // MOE_ROWS
    blk0 = (jnp.cumsum(padded) - padded) // MOE_ROWS
    nonempty = nblk > 0
    rank = jnp.cumsum(nonempty.astype(jnp.int32)) - 1
    n_nonempty = rank[-1] + 1
    has_next = nonempty & (rank + 1 < n_nonempty)
    seg_len = jnp.where(nonempty, jnp.maximum(nblk, jnp.where(has_next, W_CHUNKS, 0)), 0)
    seg_end = W_CHUNKS + jnp.cumsum(seg_len)
    seg_start = seg_end - seg_len
    total = seg_end[-1]
    by_rank = jnp.sum(jnp.where(nonempty[None, :] & (rank[None, :] == eids[:, None]), eids[None, :], 0), axis=1)

    s = jnp.arange(n_steps, dtype=jnp.int32)
    in_prologue = s < W_CHUNKS
    active = (s >= W_CHUNKS) & (s < total)
    e = jnp.minimum(jnp.sum((seg_end[None, :] <= s[:, None]).astype(jnp.int32), axis=1), n_exp - 1)
    pick = lambda v: jnp.sum(jnp.where(eids[None, :] == e[:, None], v[None, :], 0), axis=1)
    off = s - pick(seg_start)
    nblk_s, len_s = pick(nblk), pick(seg_len)
    do_compute = active & (off < nblk_s)
    last_block = (total > W_CHUNKS) * (jnp.sum(nblk) - 1)
    x_block = jnp.where(active, pick(blk0) + jnp.minimum(off, nblk_s - 1),
                        jnp.where(in_prologue, 0, last_block))
    do_convert = in_prologue | (active & (pick(has_next.astype(jnp.int32)) > 0) & (off >= len_s - W_CHUNKS))
    conv_no = jnp.maximum(jnp.cumsum(do_convert.astype(jnp.int32)) - 1, 0)
    conv_rank = conv_no // W_CHUNKS
    conv_expert = jnp.sum(jnp.where(eids[None, :] == conv_rank[:, None], by_rank[None, :], 0), axis=1)
    as_i32 = lambda v: v.astype(jnp.int32)
    return (as_i32(x_block), as_i32(do_compute), as_i32(pick(rank) % 2), as_i32(do_convert),
            as_i32(conv_expert), as_i32(conv_no % W_CHUNKS), as_i32(conv_rank % 2))


def _expert_kernel(xblk_ref, compute_ref, cslot_ref, convert_ref, cexp_ref, chunk_ref, vslot_ref,
                   x_ref, wgc_ref, wuc_ref, wdc_ref, o_ref, wg_scr, wu_scr, wd_scr):
    del xblk_ref, cexp_ref
    s = pl.program_id(0)
    ch = wgc_ref.shape[3]

    @pl.when(convert_ref[s] != 0)
    def _():
        slot = vslot_ref[s]
        for c in range(W_CHUNKS):
            @pl.when(chunk_ref[s] == c)
            def _():
                wg_scr[slot, :, c * ch:(c + 1) * ch] = wgc_ref[0, 0].astype(BF16)
                wu_scr[slot, :, c * ch:(c + 1) * ch] = wuc_ref[0, 0].astype(BF16)
                wd_scr[slot, c * ch:(c + 1) * ch, :] = wdc_ref[0, 0].astype(BF16)

    @pl.when(compute_ref[s] != 0)
    def _():
        slot = cslot_ref[s]
        x = x_ref[...].astype(BF16)
        hid = _silu(_dot(x, wg_scr[slot])) * _dot(x, wu_scr[slot])
        o_ref[...] = _dot(hid.astype(BF16), wd_scr[slot])


def _experts(layer, padded, xb, w_gate, w_up, w_down):
    n_rows, d = xb.shape
    n_exp, f = w_gate.shape[1], w_gate.shape[3]
    ch = f // W_CHUNKS
    n_steps = n_rows // MOE_ROWS + W_CHUNKS * (n_exp + 1)
    tables = _expert_schedule(padded, n_steps)
    n_tab = len(tables)
    grid_spec = pltpu.PrefetchScalarGridSpec(
        num_scalar_prefetch=n_tab,
        grid=(n_steps,),
        in_specs=[
            pl.BlockSpec((MOE_ROWS, d), lambda s, xb_, cp, cs, cv, ce, ck, vs: (xb_[s], 0)),
            pl.BlockSpec((1, 1, d, ch), lambda s, xb_, cp, cs, cv, ce, ck, vs: (layer, ce[s], 0, ck[s])),
            pl.BlockSpec((1, 1, d, ch), lambda s, xb_, cp, cs, cv, ce, ck, vs: (layer, ce[s], 0, ck[s])),
            pl.BlockSpec((1, 1, ch, d), lambda s, xb_, cp, cs, cv, ce, ck, vs: (layer, ce[s], ck[s], 0)),
        ],
        out_specs=pl.BlockSpec((MOE_ROWS, d), lambda s, xb_, cp, cs, cv, ce, ck, vs: (xb_[s], 0)),
        scratch_shapes=[
            pltpu.VMEM((2, d, f), BF16),
            pltpu.VMEM((2, d, f), BF16),
            pltpu.VMEM((2, f, d), BF16),
        ],
    )
    return pl.pallas_call(
        _expert_kernel,
        grid_spec=grid_spec,
        out_shape=jax.ShapeDtypeStruct((n_rows, d), F32),
        input_output_aliases={n_tab: 0},
        compiler_params=_cparams(("arbitrary",)),
        name="moe_experts",
    )(*tables, xb, w_gate, w_up, w_down)


def _combine_kernel(final, dest_ref, dest_next_ref, x_ref, wt_ref, g2_ref, fw_ref, yb_ref, o_ref,
                    r0_scr, r1_scr, sems):
    tm = x_ref.shape[0]
    i = pl.program_id(0)
    slot = i % 2

    def gather(idx_ref, s):
        def start(t, carry):
            _row_copy(yb_ref, idx_ref[0, t], r0_scr.at[s], t, sems.at[s]).start(priority=0)
            _row_copy(yb_ref, idx_ref[1, t], r1_scr.at[s], t, sems.at[s]).start(priority=1)
            return carry

        lax.fori_loop(0, tm, start, 0, unroll=DMA_UNROLL)

    @pl.when(i == 0)
    def _():
        gather(dest_ref, slot)

    @pl.when(i + 1 < pl.num_programs(0))
    def _():
        gather(dest_next_ref, 1 - slot)

    pltpu.make_async_copy(yb_ref.at[pl.ds(0, tm)], r0_scr.at[slot], sems.at[slot]).wait()
    pltpu.make_async_copy(yb_ref.at[pl.ds(0, tm)], r1_scr.at[slot], sems.at[slot]).wait()

    wt = wt_ref[...]
    ffn = wt[:, 0:1] * r0_scr[slot] + wt[:, 1:2] * r1_scr[slot]
    xn = x_ref[...] + g2_ref[0] * ffn
    if final:
        ms = jnp.mean(xn * xn, axis=-1, keepdims=True)
        xn = xn * lax.rsqrt(ms + EPS) * fw_ref[...]
    o_ref[...] = xn


def _combine(dest, x2, wt, g2, fw, yb, seq, final):
    n, d = x2.shape
    tm = min(TOKEN_TILE, seq)
    tpb = seq // tm
    last = n // tm - 1
    return pl.pallas_call(
        functools.partial(_combine_kernel, final),
        grid=(n // tm,),
        in_specs=[
            pl.BlockSpec((2, tm), lambda i: (0, i), memory_space=pltpu.SMEM),
            pl.BlockSpec((2, tm), lambda i: (0, jnp.minimum(i + 1, last)), memory_space=pltpu.SMEM),
            pl.BlockSpec((tm, d), lambda i: (i, 0)),
            pl.BlockSpec((tm, 2), lambda i: (i, 0)),
            pl.BlockSpec((1, 1, d), lambda i: (i // tpb, 0, 0)),
            pl.BlockSpec((1, d), lambda i: (0, 0)),
            pl.BlockSpec(memory_space=pl.ANY),
        ],
        out_specs=pl.BlockSpec((tm, d), lambda i: (i, 0)),
        out_shape=jax.ShapeDtypeStruct((n, d), F32),
        scratch_shapes=[
            pltpu.VMEM((2, tm, d), F32),
            pltpu.VMEM((2, tm, d), F32),
            pltpu.SemaphoreType.DMA((2,)),
        ],
        compiler_params=_cparams(("arbitrary",)),
        name="moe_combine",
    )(dest, dest, x2, wt, g2, fw, yb)


def _layer_params(l, d, w_in, conv_w, conv_b, dt_bias, a_log, d_skip, ssm_norm_w,
                  lb_all, hgrn_norm_w):
    gn = SSM_GROUPS * SSM_STATE
    n_heads = d // SSM_HEAD_DIM
    w_r, w_dt = _relayout_w_in(w_in, l)
    pad_h = lambda v: jnp.pad(v.astype(F32), (0, LANES - n_heads)).reshape(1, LANES)
    lb = jnp.maximum(lb_all[l], 0.0)
    head_of_col = jnp.arange(d, dtype=jnp.int32) // SSM_HEAD_DIM
    head_expand = (jnp.arange(LANES, dtype=jnp.int32)[:, None] == head_of_col[None, :]).astype(BF16)
    cw, cb = conv_w[l], conv_b[l]
    return dict(
        w_r=w_r, w_dt=w_dt,
        cwx=cw[:, :d], cwb=cw[:, d:d + gn], cwc=cw[:, d + gn:],
        cbx=cb[:d].reshape(1, d), cbb=cb[d:d + gn].reshape(1, gn), cbc=cb[d + gn:].reshape(1, gn),
        dt_bias=pad_h(dt_bias[l]), a_log=pad_h(a_log[l]),
        d_skip=jnp.repeat(d_skip[l].astype(F32), SSM_HEAD_DIM).reshape(1, d),
        ssm_norm_w=ssm_norm_w[l].reshape(1, d),
        head_expand=head_expand,
        log_lb=(jnp.log(lb) * LOG2E).reshape(1, d), log_1m_lb=(jnp.log1p(-lb) * LOG2E).reshape(1, d),
        hgrn_norm_w=hgrn_norm_w[l].reshape(1, d),
    )


def kernel(x, c, ada_w, ada_b, norm1_w, norm2_w, w_in, conv_w, conv_b, dt_bias, a_log, d_skip,
           ssm_norm_w, hgrn_lower_bounds, hgrn_norm_w, w_out, router_w, router_bias,
           w_gate, w_up, w_down, final_norm_w):
    bsz, seq, d = x.shape
    n_layers = ada_w.shape[0]
    n_exp = router_w.shape[1]
    n = bsz * seq
    n_assign = 2 * n
    nb = n_assign // MOE_ROWS + n_exp
    n_rows = nb * MOE_ROWS

    lb_all = jnp.cumsum(jax.nn.softmax(hgrn_lower_bounds.astype(F32), axis=0), axis=0)
    lb_all = lb_all - lb_all[0:1]
    mod = _modulation(c, ada_w, ada_b)
    rwt = router_w.astype(F32).T
    rb = router_bias.astype(F32).reshape(n_exp, 1)
    fw = final_norm_w.reshape(1, d)
    w_in = w_in.astype(BF16)

    x2 = x.reshape(n, d)
    for l in range(n_layers):
        sh1, sc1, g1, sh2, sc2, g2 = [mod[l, :, k * d:(k + 1) * d].reshape(bsz, 1, d) for k in range(6)]
        p = _layer_params(l, d, w_in, conv_w, conv_b, dt_bias, a_log, d_skip, ssm_norm_w,
                          lb_all, hgrn_norm_w)
        proj, dt_raw = _in_projection(x2, norm1_w[l].reshape(1, d), sc1, sh1, p["w_r"], p["w_dt"], seq)
        ya = _ssd_branch(proj, dt_raw, p, bsz, seq, d)
        yb = _hgrn_branch(proj, p, bsz, seq, d)
        x2, h2, mi, mw, cnt = _out_projection(
            x2, ya, yb, proj, w_out[l].astype(BF16), g1, norm2_w[l].reshape(1, d), sc2, sh2, rwt, rb, seq)

        counts = cnt[:, 0].astype(jnp.int32)
        padded = (counts + MOE_ROWS - 1) // MOE_ROWS * MOE_ROWS
        pend = jnp.cumsum(padded)
        pstart = pend - padded
        is_e = mi[0:2, :, None] == jnp.arange(n_exp, dtype=jnp.int32)
        dest = jnp.sum(jnp.where(is_e, pstart, 0), axis=-1) + mi[2:4]

        xb = _dispatch(pend.astype(jnp.int32), dest, h2, n_rows)
        yexp = _experts(l, padded, xb, w_gate, w_up, w_down)
        x2 = _combine(dest, x2, mw.T, g2, fw, yexp, seq, final=(l == n_layers - 1))
    return x2.reshape(bsz, seq, d)
```
